```python
import jax, jax.numpy as jnp
from jax import lax
import numpy as np

D_MODEL = 1024
BATCH = 4
SEQ = 8192
DEPTH = 1

D_RNN = 1024
RNN_BLOCKS = 8
RNN_BLOCK_W = D_RNN // RNN_BLOCKS
CONV_W = 4
LRU_C = 8.0
N_HEADS = 8
QK_NOPE = 128
QK_ROPE = 64
QK_DIM = QK_NOPE + QK_ROPE
V_DIM = 128
Q_LORA = 256
KV_LORA = 256
ROPE_THETA = 10000.0
Q_BLOCK = 128
N_GROUPS = 8
EXPERTS_PER_GROUP = 8
N_EXPERTS = N_GROUPS * EXPERTS_PER_GROUP
TOP_K = 2
D_EXPERT = 256
MOE_BLOCK = 128
EPS = 1e-6

IN_SPLITS = (D_RNN, D_RNN, Q_LORA, KV_LORA, QK_ROPE, D_MODEL, D_MODEL)
D_IN = D_RNN + D_RNN + Q_LORA + KV_LORA + QK_ROPE + D_MODEL + D_MODEL

kernel_name = "hybrid_rglru_mla_hiermoe_block"


def rmsnorm(x, g):
    xf = x.astype(jnp.float32)
    y = xf * lax.rsqrt(jnp.mean(xf * xf, axis=-1, keepdims=True) + EPS)
    return (y * g.astype(jnp.float32)).astype(x.dtype)


def causal_depthwise_conv(x, w, b):
    y = lax.conv_general_dilated(
        x, w[:, None, :].astype(x.dtype), window_strides=(1,),
        padding=[(CONV_W - 1, 0)], dimension_numbers=("NWC", "WIO", "NWC"),
        feature_group_count=x.shape[-1])
    return y + b.astype(x.dtype)


def rg_lru(x, wa, ba, wx, bx, lam):
    B, S, C = x.shape
    xb = x.reshape(B, S, RNN_BLOCKS, RNN_BLOCK_W)
    r = jax.nn.sigmoid(jnp.einsum("bsnk,nkj->bsnj", xb, wa).reshape(B, S, C) + ba)
    i = jax.nn.sigmoid(jnp.einsum("bsnk,nkj->bsnj", xb, wx).reshape(B, S, C) + bx)
    log_a = -LRU_C * r.astype(jnp.float32) * jax.nn.softplus(-lam.astype(jnp.float32))
    a = jnp.exp(log_a)
    mult = jnp.sqrt(-jnp.expm1(2.0 * log_a))
    b_in = x.astype(jnp.float32) * i.astype(jnp.float32) * mult

    def combine(left, right):
        a1, b1 = left
        a2, b2 = right
        return a1 * a2, a2 * b1 + b2

    _, h = lax.associative_scan(combine, (a, b_in), axis=1)
    return h.astype(x.dtype)


def apply_rope(x, cos, sin):
    half = x.shape[-1] // 2
    x1, x2 = x[..., :half], x[..., half:]
    return jnp.concatenate([x1 * cos - x2 * sin, x2 * cos + x1 * sin], axis=-1)


def mla_attention(c_q, c_kv, k_pe, positions, q_norm_g, w_uq, kv_norm_g, w_ukv,
                  qk_q_g, qk_k_g, w_mla_o):
    B, S, _ = c_q.shape
    q = (rmsnorm(c_q, q_norm_g) @ w_uq).reshape(B, S, N_HEADS, QK_DIM)
    kv = (rmsnorm(c_kv, kv_norm_g) @ w_ukv).reshape(B, S, N_HEADS, QK_NOPE + V_DIM)
    k_nope, v = kv[..., :QK_NOPE], kv[..., QK_NOPE:]
    k_pe_h = jnp.broadcast_to(k_pe[:, :, None, :], (B, S, N_HEADS, QK_ROPE))
    k = jnp.concatenate([k_nope, k_pe_h], axis=-1)
    q = rmsnorm(q, qk_q_g)
    k = rmsnorm(k, qk_k_g)
    inv_freq = ROPE_THETA ** (-jnp.arange(0, QK_ROPE, 2, dtype=jnp.float32) / QK_ROPE)
    ang = positions.astype(jnp.float32)[..., None] * inv_freq
    cos = jnp.cos(ang)[:, :, None, :].astype(q.dtype)
    sin = jnp.sin(ang)[:, :, None, :].astype(q.dtype)
    q = jnp.concatenate([q[..., :QK_NOPE], apply_rope(q[..., QK_NOPE:], cos, sin)], axis=-1)
    k = jnp.concatenate([k[..., :QK_NOPE], apply_rope(k[..., QK_NOPE:], cos, sin)], axis=-1)
    scale = QK_DIM ** -0.5
    outs = []
    for s0 in range(0, S, Q_BLOCK):
        e = s0 + Q_BLOCK
        s = jnp.einsum("bqhd,bkhd->bhqk", q[:, s0:e], k[:, :e]).astype(jnp.float32) * scale
        mask = jnp.arange(e)[None, :] <= (s0 + jnp.arange(Q_BLOCK))[:, None]
        p = jax.nn.softmax(jnp.where(mask, s, -jnp.inf), axis=-1).astype(v.dtype)
        outs.append(jnp.einsum("bhqk,bkhd->bqhd", p, v[:, :e]))
    o = jnp.concatenate(outs, axis=1).reshape(B, S, N_HEADS * V_DIM)
    return o @ w_mla_o


def hier_moe(u, wg, bg, we, be, w1, w3, w2):
    B, S, D = u.shape
    N = B * S
    xf = u.reshape(N, D)
    g_logits = (xf @ wg).astype(jnp.float32) + bg.astype(jnp.float32)
    g_prob = jax.nn.softmax(g_logits, axis=-1)
    g_sel = jnp.argmax(g_logits, axis=-1).astype(jnp.int32)
    e_logits = ((xf @ we).astype(jnp.float32) + be.astype(jnp.float32)).reshape(N, N_GROUPS, EXPERTS_PER_GROUP)
    e_in_group = jnp.take_along_axis(e_logits, g_sel[:, None, None], axis=1)[:, 0]
    e_prob = jax.nn.softmax(e_in_group, axis=-1)
    top_p, top_j = lax.top_k(e_prob, TOP_K)
    gate = jnp.take_along_axis(g_prob, g_sel[:, None], axis=1) * top_p / jnp.sum(top_p, axis=-1, keepdims=True)
    expert_id = g_sel[:, None] * EXPERTS_PER_GROUP + top_j.astype(jnp.int32)
    M = N * TOP_K
    flat_e = expert_id.reshape(M)
    flat_tok = jnp.repeat(jnp.arange(N, dtype=jnp.int32), TOP_K)
    flat_w = gate.reshape(M)
    order = jnp.argsort(flat_e)
    sorted_e = flat_e[order]
    counts = jnp.bincount(flat_e, length=N_EXPERTS)
    padded = ((counts + MOE_BLOCK - 1) // MOE_BLOCK) * MOE_BLOCK
    start = jnp.cumsum(counts) - counts
    pad_end = jnp.cumsum(padded)
    pad_start = pad_end - padded
    dest = pad_start[sorted_e] + (jnp.arange(M, dtype=jnp.int32) - start[sorted_e])
    m_pad = M + N_EXPERTS * MOE_BLOCK
    n_blocks = m_pad // MOE_BLOCK
    tok_buf = jnp.full((m_pad,), N, dtype=jnp.int32).at[dest].set(flat_tok[order])
    w_buf = jnp.zeros((m_pad,), dtype=jnp.float32).at[dest].set(flat_w[order])
    block_e = jnp.minimum(
        jnp.searchsorted(pad_end, jnp.arange(n_blocks, dtype=jnp.int32) * MOE_BLOCK, side="right"),
        N_EXPERTS - 1).astype(jnp.int32)
    x_pad = jnp.concatenate([xf, jnp.zeros((1, D), xf.dtype)], axis=0)[tok_buf]
    x_pad = x_pad.reshape(n_blocks, MOE_BLOCK, D)

    def expert_block(args):
        xb, e = args
        h = jax.nn.silu(xb @ w1[e]) * (xb @ w3[e])
        return h @ w2[e]

    y_pad = lax.map(expert_block, (x_pad, block_e)).reshape(m_pad, D)
    y = jax.ops.segment_sum(y_pad * w_buf[:, None].astype(y_pad.dtype), tok_buf, num_segments=N + 1)[:N]
    return y.reshape(B, S, D)


def setup_inputs(seed: int = 0) -> dict:
    key = jax.random.key(seed)
    ks = jax.random.split(key, 32)
    L, D, f32 = DEPTH, D_MODEL, jnp.float32

    def nrm(k, shape, fan_in):
        return jax.random.normal(k, shape, f32) * (fan_in ** -0.5)

    def gain(k, shape):
        return 1.0 + 0.1 * jax.random.normal(k, shape, f32)

    def bias(k, shape):
        return 0.01 * jax.random.normal(k, shape, f32)

    x = jax.random.normal(ks[0], (BATCH, SEQ, D), f32)
    offset = jax.random.randint(ks[1], (BATCH, 1), 0, 4096, dtype=jnp.int32)
    positions = offset + jnp.arange(SEQ, dtype=jnp.int32)[None, :]
    a0 = jax.random.uniform(ks[2], (L, D_RNN), f32, 0.9, 0.999)
    s0 = a0 ** (1.0 / LRU_C)
    lru_lambda = jnp.log(s0) - jnp.log1p(-s0)
    return {
        "x": x,
        "positions": positions,
        "norm1_g": gain(ks[3], (L, D)),
        "w_in": nrm(ks[4], (L, D, D_IN), D),
        "conv_w": nrm(ks[5], (L, CONV_W, D_RNN), CONV_W),
        "conv_b": bias(ks[6], (L, D_RNN)),
        "lru_wa": nrm(ks[7], (L, RNN_BLOCKS, RNN_BLOCK_W, RNN_BLOCK_W), RNN_BLOCK_W),
        "lru_ba": bias(ks[8], (L, D_RNN)),
        "lru_wx": nrm(ks[9], (L, RNN_BLOCKS, RNN_BLOCK_W, RNN_BLOCK_W), RNN_BLOCK_W),
        "lru_bx": bias(ks[10], (L, D_RNN)),
        "lru_lambda": lru_lambda,
        "w_rnn_o": nrm(ks[11], (L, D_RNN, D), D_RNN),
        "q_norm_g": gain(ks[12], (L, Q_LORA)),
        "w_uq": nrm(ks[13], (L, Q_LORA, N_HEADS * QK_DIM), Q_LORA),
        "kv_norm_g": gain(ks[14], (L, KV_LORA)),
        "w_ukv": nrm(ks[15], (L, KV_LORA, N_HEADS * (QK_NOPE + V_DIM)), KV_LORA),
        "qk_norm_q_g": gain(ks[16], (L, QK_DIM)),
        "qk_norm_k_g": gain(ks[17], (L, QK_DIM)),
        "w_mla_o": nrm(ks[18], (L, N_HEADS * V_DIM, D), N_HEADS * V_DIM),
        "w_out": nrm(ks[19], (L, D, D), D),
        "norm2_g": gain(ks[20], (L, D)),
        "router_wg": nrm(ks[21], (L, D, N_GROUPS), D),
        "router_bg": bias(ks[22], (L, N_GROUPS)),
        "router_we": nrm(ks[23], (L, D, N_EXPERTS), D),
        "router_be": bias(ks[24], (L, N_EXPERTS)),
        "exp_w1": nrm(ks[25], (L, N_EXPERTS, D, D_EXPERT), D),
        "exp_w3": nrm(ks[26], (L, N_EXPERTS, D, D_EXPERT), D),
        "exp_w2": nrm(ks[27], (L, N_EXPERTS, D_EXPERT, D), D_EXPERT),
    }


def reference(x, positions, norm1_g, w_in, conv_w, conv_b, lru_wa, lru_ba, lru_wx, lru_bx,
              lru_lambda, w_rnn_o, q_norm_g, w_uq, kv_norm_g, w_ukv, qk_norm_q_g, qk_norm_k_g,
              w_mla_o, w_out, norm2_g, router_wg, router_bg, router_we, router_be,
              exp_w1, exp_w3, exp_w2):
    split_points = [int(p) for p in np.cumsum(IN_SPLITS)[:-1]]
    h = x
    for l in range(DEPTH):
        u = rmsnorm(h, norm1_g[l])
        proj = u @ w_in[l]
        x_r, y_r, c_q, c_kv, k_pe, g_a, g_b = jnp.split(proj, split_points, axis=-1)
        x_r = causal_depthwise_conv(x_r, conv_w[l], conv_b[l])
        h_r = rg_lru(x_r, lru_wa[l], lru_ba[l], lru_wx[l], lru_bx[l], lru_lambda[l])
        branch_a = (h_r * jax.nn.gelu(y_r, approximate=True)) @ w_rnn_o[l]
        branch_b = mla_attention(c_q, c_kv, k_pe, positions, q_norm_g[l], w_uq[l], kv_norm_g[l],
                                 w_ukv[l], qk_norm_q_g[l], qk_norm_k_g[l], w_mla_o[l])
        merged = jax.nn.sigmoid(g_a) * branch_a + jax.nn.sigmoid(g_b) * branch_b
        h = h + merged @ w_out[l]
        h = h + hier_moe(rmsnorm(h, norm2_g[l]), router_wg[l], router_bg[l], router_we[l],
                         router_be[l], exp_w1[l], exp_w3[l], exp_w2[l])
    return h
```

```python
import functools
import math

import numpy as np
import jax
import jax.numpy as jnp
from jax import lax
from jax.experimental import pallas as pl
from jax.experimental.pallas import tpu as pltpu

F32 = jnp.float32
BF16 = jnp.bfloat16
EPS = 1e-6

N_HEADS = 8
QK_NOPE = 128
QK_ROPE = 64
QK_DIM = QK_NOPE + QK_ROPE
V_DIM = 128
ROPE_THETA = 10000.0
LRU_C = 8.0
TOP_K = 2
MOE_BLOCK = 128
ROUTER_LANES = 128

SUBLANES = 8
LANES = 128
VMEM_LIMIT_BYTES = 56 * 1024 * 1024

T_INPROJ = 256
T_RGLRU = 256
T_MLAPROJ = 256
T_Q = 512
T_KV = 512
T_MERGE = 256
T_COMBINE = 256


def _params(*sem):
    return pltpu.CompilerParams(dimension_semantics=sem, vmem_limit_bytes=VMEM_LIMIT_BYTES)


def _sigmoid(z):
    return 1.0 / (1.0 + jnp.exp(-z))


def _const_spec(shape):
    n = len(shape)
    return pl.BlockSpec(shape, lambda *_: (0,) * n)


def _inproj_body(x_ref, g_ref, w_ref, *out_refs):
    x = x_ref[...]
    u = x * lax.rsqrt(jnp.mean(x * x, axis=-1, keepdims=True) + EPS) * g_ref[...]
    u = u.astype(BF16)
    off = 0
    for ref in out_refs:
        n = ref.shape[-1]
        ref[...] = jnp.dot(u, w_ref[:, off:off + n], preferred_element_type=F32).astype(ref.dtype)
        off += n


def _in_proj(x2, g, w, widths, dtypes):
    n, d = x2.shape
    t = T_INPROJ
    return pl.pallas_call(
        _inproj_body,
        grid=(n // t,),
        in_specs=[pl.BlockSpec((t, d), lambda i: (i, 0)), _const_spec(g.shape), _const_spec(w.shape)],
        out_specs=[pl.BlockSpec((t, c), lambda i: (i, 0)) for c in widths],
        out_shape=[jax.ShapeDtypeStruct((n, c), dt) for c, dt in zip(widths, dtypes)],
        compiler_params=_params("parallel"),
        name="in_proj",
    )(x2, g, w)


def _rglru_body(xr_ref, yr_ref, ga_ref, cw_ref, cb_ref, wg_ref, ba_ref, bx_ref, lam_ref, wo_ref,
                out_ref, xbuf, a_s, b_s, h_s, hcar):
    t, c = xr_ref.shape
    nblk, bw, _ = wg_ref.shape

    @pl.when(pl.program_id(1) == 0)
    def _():
        xbuf[0:SUBLANES, :] = jnp.zeros((SUBLANES, c), F32)
        hcar[...] = jnp.zeros_like(hcar)

    xbuf[SUBLANES:, :] = xr_ref[...].astype(F32)
    lam = lam_ref[...]
    sp = jnp.maximum(-lam, 0.0) + jnp.log(1.0 + jnp.exp(-jnp.abs(lam)))
    rowmod = lax.broadcasted_iota(jnp.int32, (t, bw), 0) % SUBLANES

    for n in range(nblk):
        sl = slice(n * bw, (n + 1) * bw)
        xc = cb_ref[:, sl] + cw_ref[3:4, sl] * xbuf[SUBLANES:SUBLANES + t, sl]
        for s in (1, 2, 3):
            xc = xc + cw_ref[3 - s:4 - s, sl] * xbuf[SUBLANES - s:SUBLANES - s + t, sl]
        g = jnp.dot(xc.astype(BF16), wg_ref[n], preferred_element_type=F32)
        r = _sigmoid(g[:, :bw] + ba_ref[:, sl])
        i = _sigmoid(g[:, bw:] + bx_ref[:, sl])
        log_a = -LRU_C * r * sp[:, sl]
        a = jnp.exp(log_a)
        b = xc * i * jnp.sqrt(-jnp.tanh(log_a) * (a * a + 1.0))
        for d in (1, 2, 4):
            keep = rowmod >= d
            a_sh = pltpu.roll(a, d, 0)
            b_sh = pltpu.roll(b, d, 0)
            b = jnp.where(keep, a * b_sh + b, b)
            a = jnp.where(keep, a * a_sh, a)
        a_s[:, sl] = a
        b_s[:, sl] = b

    xbuf[0:SUBLANES, :] = xbuf[t:t + SUBLANES, :]

    def group(gi, hb):
        rows = pl.ds(pl.multiple_of(gi * SUBLANES, SUBLANES), SUBLANES)
        h8 = a_s[rows, :] * hb + b_s[rows, :]
        h_s[rows, :] = h8
        return jnp.broadcast_to(h8[SUBLANES - 1:SUBLANES, :], (SUBLANES, c))

    hcar[...] = lax.fori_loop(0, t // SUBLANES, group, hcar[...])

    y = yr_ref[...].astype(F32)
    gelu = 0.5 * y * (1.0 + jnp.tanh(math.sqrt(2.0 / math.pi) * (y + 0.044715 * (y * y * y))))
    hg = (h_s[...] * gelu).astype(BF16)
    br = jnp.dot(hg, wo_ref[...], preferred_element_type=F32)
    out_ref[...] = (_sigmoid(ga_ref[...].astype(F32)) * br).astype(out_ref.dtype)


def _rglru(xr, yr, ga, cw, cb, wg, ba, bx, lam, wo):
    b, s, c = xr.shape
    t = T_RGLRU
    tile = pl.BlockSpec((None, t, c), lambda bi, j: (bi, j, 0))
    consts = (cw, cb, wg, ba, bx, lam, wo)
    return pl.pallas_call(
        _rglru_body,
        grid=(b, s // t),
        in_specs=[tile, tile, tile] + [_const_spec(a.shape) for a in consts],
        out_specs=tile,
        out_shape=jax.ShapeDtypeStruct((b, s, c), BF16),
        scratch_shapes=[
            pltpu.VMEM((t + SUBLANES, c), F32),
            pltpu.VMEM((t, c), F32),
            pltpu.VMEM((t, c), F32),
            pltpu.VMEM((t, c), F32),
            pltpu.VMEM((SUBLANES, c), F32),
        ],
        compiler_params=_params("parallel", "arbitrary"),
        name="rglru",
    )(xr, yr, ga, *consts)


def _mlaproj_body(cq_ref, ckv_ref, kpe_ref, pos_ref, qg_ref, kvg_ref, wq_ref, wqr_ref, wkv_ref,
                  gq_ref, gqr_ref, gkn_ref, gkr_ref, gkrr_ref, freq_ref, q_ref, k_ref, v_ref):
    nh = wq_ref.shape[0]
    ang = pos_ref[...] * freq_ref[...]
    cos = jnp.cos(ang)
    sin = jnp.sin(ang)

    def latent_norm(ref, g_ref):
        z = ref[...]
        return (z * lax.rsqrt(jnp.mean(z * z, axis=-1, keepdims=True) + EPS) * g_ref[...]).astype(BF16)

    cqn = latent_norm(cq_ref, qg_ref)
    ckvn = latent_norm(ckv_ref, kvg_ref)
    scale = QK_DIM ** -0.5
    gq_cos = gq_ref[...] * cos
    gq_sin = gqr_ref[...] * sin
    for h in range(nh):
        qh = jnp.dot(cqn, wq_ref[h], preferred_element_type=F32)
        qr = jnp.dot(cqn, wqr_ref[h], preferred_element_type=F32)
        rstd = lax.rsqrt(jnp.mean(qh * qh, axis=-1, keepdims=True) + EPS)
        q_ref[h] = ((qh * gq_cos + qr * gq_sin) * (rstd * scale)).astype(q_ref.dtype)

    kpe = kpe_ref[:, :QK_ROPE]
    kpr = kpe_ref[:, QK_ROPE:]
    k_rot = kpe * gkr_ref[...] * cos[:, QK_NOPE:] + kpr * gkrr_ref[...] * sin[:, QK_NOPE:]
    ss_pe = jnp.sum(kpe * kpe, axis=-1, keepdims=True)
    for h in range(nh):
        kv = jnp.dot(ckvn, wkv_ref[h], preferred_element_type=F32)
        kn = kv[:, :QK_NOPE]
        rstd = lax.rsqrt((jnp.sum(kn * kn, axis=-1, keepdims=True) + ss_pe) * (1.0 / QK_DIM) + EPS)
        k_ref[h, :, :QK_NOPE] = (kn * gkn_ref[...] * rstd).astype(k_ref.dtype)
        k_ref[h, :, QK_NOPE:] = (k_rot * rstd).astype(k_ref.dtype)
        v_ref[h] = kv[:, QK_NOPE:].astype(v_ref.dtype)


def _mla_proj(cq, ckv, kpe2, pos, consts):
    b, s, _ = cq.shape
    t = T_MLAPROJ
    nh = N_HEADS

    def tile(c):
        return pl.BlockSpec((None, t, c), lambda bi, j: (bi, j, 0))

    def head_tile(c):
        return pl.BlockSpec((None, nh, t, c), lambda bi, j: (bi, 0, j, 0))

    return pl.pallas_call(
        _mlaproj_body,
        grid=(b, s // t),
        in_specs=[tile(cq.shape[-1]), tile(ckv.shape[-1]), tile(kpe2.shape[-1]), tile(1)]
        + [_const_spec(a.shape) for a in consts],
        out_specs=[head_tile(QK_DIM), head_tile(QK_DIM), head_tile(V_DIM)],
        out_shape=[jax.ShapeDtypeStruct((b, nh, s, QK_DIM), BF16),
                   jax.ShapeDtypeStruct((b, nh, s, QK_DIM), BF16),
                   jax.ShapeDtypeStruct((b, nh, s, V_DIM), BF16)],
        compiler_params=_params("parallel", "parallel"),
        name="mla_proj",
    )(cq, ckv, kpe2, pos, *consts)


def _attn_body(q_ref, k_ref, v_ref, o_ref, m_s, l_s, acc_s):
    tq = q_ref.shape[0]
    tk = T_KV
    qi = pl.program_id(2)
    q = q_ref[...]
    m_s[...] = jnp.full(m_s.shape, -jnp.inf, F32)
    l_s[...] = jnp.zeros(l_s.shape, F32)
    acc_s[...] = jnp.zeros(acc_s.shape, F32)

    def step(k0, masked):
        k = k_ref[pl.ds(k0, tk), :]
        v = v_ref[pl.ds(k0, tk), :]
        s = lax.dot_general(q, k, (((1,), (1,)), ((), ())), preferred_element_type=F32)
        if masked:
            row = lax.broadcasted_iota(jnp.int32, (tq, tk), 0)
            col = lax.broadcasted_iota(jnp.int32, (tq, tk), 1)
            s = jnp.where(col + (k0 - qi * tq) <= row, s, -jnp.inf)
        m_old = m_s[...]
        m_new = jnp.maximum(m_old, jnp.max(s, axis=-1, keepdims=True))
        alpha = jnp.exp(m_old - m_new)
        p = jnp.exp(s - m_new)
        l_s[...] = alpha * l_s[...] + jnp.sum(p, axis=-1, keepdims=True)
        acc_s[...] = alpha * acc_s[...] + jnp.dot(p.astype(v.dtype), v, preferred_element_type=F32)
        m_s[...] = m_new

    def full_block(j, carry):
        step(pl.multiple_of(j * tk, tk), False)
        return carry

    lax.fori_loop(0, qi * (tq // tk), full_block, 0)
    for j in range(tq // tk):
        step(pl.multiple_of(qi * tq + j * tk, tk), True)
    o_ref[...] = (acc_s[...] / l_s[...]).astype(o_ref.dtype)


def _attention(q, k, v):
    b, nh, s, dqk = q.shape
    dv = v.shape[-1]
    tq = T_Q
    return pl.pallas_call(
        _attn_body,
        grid=(b, nh, s // tq),
        in_specs=[pl.BlockSpec((None, None, tq, dqk), lambda bi, h, i: (bi, h, i, 0)),
                  pl.BlockSpec((None, None, s, dqk), lambda bi, h, i: (bi, h, 0, 0)),
                  pl.BlockSpec((None, None, s, dv), lambda bi, h, i: (bi, h, 0, 0))],
        out_specs=pl.BlockSpec((None, tq, dv), lambda bi, h, i: (bi, i, h)),
        out_shape=jax.ShapeDtypeStruct((b, s, nh * dv), BF16),
        scratch_shapes=[pltpu.VMEM((tq, 1), F32), pltpu.VMEM((tq, 1), F32), pltpu.VMEM((tq, dv), F32)],
        compiler_params=_params("parallel", "parallel", "arbitrary"),
        name="attention",
    )(q, k, v)


def _merge_body(o_ref, ga_ref, gb_ref, x_ref, wmo_ref, wout_ref, g2_ref, wrh_ref, wrl_ref, rb_ref,
                h_ref, rec_ref, *, n_groups, per_group):
    br_b = jnp.dot(o_ref[...], wmo_ref[...], preferred_element_type=F32)
    merged = ga_ref[...].astype(F32) + _sigmoid(gb_ref[...].astype(F32)) * br_b
    h = x_ref[...] + jnp.dot(merged.astype(BF16), wout_ref[...], preferred_element_type=F32)
    h_ref[...] = h

    u = h * lax.rsqrt(jnp.mean(h * h, axis=-1, keepdims=True) + EPS) * g2_ref[...]
    u_hi = u.astype(BF16)
    u_lo = (u - u_hi.astype(F32)).astype(BF16)
    logits = (jnp.dot(u_hi, wrh_ref[...], preferred_element_type=F32)
              + jnp.dot(u_lo, wrh_ref[...], preferred_element_type=F32)
              + jnp.dot(u_hi, wrl_ref[...], preferred_element_type=F32)) + rb_ref[...]

    t, nl = logits.shape
    lane = lax.broadcasted_iota(jnp.int32, (t, nl), 1)
    neg = -jnp.inf

    def first_argmax(z, zmax):
        return jnp.min(jnp.where(z == zmax, lane, nl), axis=-1, keepdims=True)

    zg = jnp.where(lane < n_groups, logits, neg)
    mg = jnp.max(zg, axis=-1, keepdims=True)
    g_sel = first_argmax(zg, mg)
    g_prob = 1.0 / jnp.sum(jnp.exp(zg - mg), axis=-1, keepdims=True)
    lo = n_groups + g_sel * per_group
    ze = jnp.where((lane >= lo) & (lane < lo + per_group), logits, neg)
    m1 = jnp.max(ze, axis=-1, keepdims=True)
    i1 = first_argmax(ze, m1)
    ze2 = jnp.where(lane == i1, neg, ze)
    m2 = jnp.max(ze2, axis=-1, keepdims=True)
    i2 = first_argmax(ze2, m2)
    e21 = jnp.exp(m2 - m1)
    w1 = g_prob / (1.0 + e21)
    w2 = w1 * e21
    rec = jnp.where(lane == 0, (i1 - n_groups).astype(F32), 0.0)
    rec = jnp.where(lane == 1, (i2 - n_groups).astype(F32), rec)
    rec = jnp.where(lane == 2, w1, rec)
    rec = jnp.where(lane == 3, w2, rec)
    rec_ref[...] = rec


def _merge(o2, ga2, gb2, x2, wmo, wout, g2, wrh, wrl, rb, n_groups, per_group):
    n, d = x2.shape
    t = T_MERGE
    consts = (wmo, wout, g2, wrh, wrl, rb)
    tile = pl.BlockSpec((t, d), lambda i: (i, 0))
    return pl.pallas_call(
        functools.partial(_merge_body, n_groups=n_groups, per_group=per_group),
        grid=(n // t,),
        in_specs=[tile, tile, tile, tile] + [_const_spec(a.shape) for a in consts],
        out_specs=[tile, pl.BlockSpec((t, ROUTER_LANES), lambda i: (i, 0))],
        out_shape=[jax.ShapeDtypeStruct((n, d), F32), jax.ShapeDtypeStruct((n, ROUTER_LANES), F32)],
        compiler_params=_params("parallel"),
        name="merge_router",
    )(o2, ga2, gb2, x2, *consts)


def _row_gather_copy(src_hbm, row, dst, dst_row, sem):
    return pltpu.make_async_copy(src_hbm.at[pl.ds(row, 1), :], dst.at[pl.ds(dst_row, 1), :], sem)


def _experts_body(tok_ref, be_ref, nb_ref, h_hbm, g2_ref, wrow_ref, w1_ref, w3_ref, w2_ref, y_ref, xbuf, sems):
    blk = pl.program_id(0)
    nblk = pl.num_programs(0)
    rows = xbuf.shape[1]
    n_used = nb_ref[0]

    def issue(b, slot):
        def body(r, carry):
            _row_gather_copy(h_hbm, tok_ref[b * rows + r], xbuf.at[slot], r, sems.at[slot]).start()
            return carry
        lax.fori_loop(0, rows, body, 0)

    def wait_all(slot):
        def body(r, carry):
            _row_gather_copy(h_hbm, 0, xbuf.at[slot], r, sems.at[slot]).wait()
            return carry
        lax.fori_loop(0, rows, body, 0)

    slot = blk % 2

    @pl.when((blk == 0) & (n_used > 0))
    def _():
        issue(0, 0)

    @pl.when(blk + 1 < jnp.minimum(n_used, nblk))
    def _():
        issue(blk + 1, 1 - slot)

    @pl.when(blk < n_used)
    def _():
        wait_all(slot)
        x = xbuf[slot]
        u = (x * lax.rsqrt(jnp.mean(x * x, axis=-1, keepdims=True) + EPS) * g2_ref[...]).astype(BF16)
        a = jnp.dot(u, w1_ref[...], preferred_element_type=F32)
        c = jnp.dot(u, w3_ref[...], preferred_element_type=F32)
        hmid = (a * _sigmoid(a) * c).astype(BF16)
        y = jnp.dot(hmid, w2_ref[...], preferred_element_type=F32)
        y_ref[...] = y * wrow_ref[...]

    @pl.when(blk >= n_used)
    def _():
        y_ref[...] = jnp.zeros_like(y_ref)


def _experts(tok_buf, block_e, n_used, h1, g2, w_rows, w1, w3, w2):
    n, d = h1.shape
    m_pad = tok_buf.shape[0]
    rows = MOE_BLOCK
    nblk = m_pad // rows
    de = w1.shape[-1]
    grid_spec = pltpu.PrefetchScalarGridSpec(
        num_scalar_prefetch=3,
        grid=(nblk,),
        in_specs=[
            pl.BlockSpec(memory_space=pl.ANY),
            pl.BlockSpec(g2.shape, lambda i, *_: (0, 0)),
            pl.BlockSpec((rows, 1), lambda i, *_: (i, 0)),
            pl.BlockSpec((None, d, de), lambda i, tok, be, nb: (be[i], 0, 0)),
            pl.BlockSpec((None, d, de), lambda i, tok, be, nb: (be[i], 0, 0)),
            pl.BlockSpec((None, de, d), lambda i, tok, be, nb: (be[i], 0, 0)),
        ],
        out_specs=pl.BlockSpec((rows, d), lambda i, *_: (i, 0)),
        scratch_shapes=[pltpu.VMEM((2, rows, d), F32), pltpu.SemaphoreType.DMA((2,))],
    )
    return pl.pallas_call(
        _experts_body,
        grid_spec=grid_spec,
        out_shape=jax.ShapeDtypeStruct((m_pad, d), F32),
        compiler_params=_params("arbitrary"),
        name="experts",
    )(tok_buf, block_e, n_used, h1, g2, w_rows, w1, w3, w2)


def _combine_body(dest_ref, h_ref, y_hbm, out_ref, ybuf, sems):
    i = pl.program_id(0)
    nsteps = pl.num_programs(0)
    t = h_ref.shape[0]
    nrows = TOP_K * t

    def issue(step, slot):
        def body(r, carry):
            _row_gather_copy(y_hbm, dest_ref[step * nrows + r], ybuf.at[slot], r, sems.at[slot]).start()
            return carry
        lax.fori_loop(0, nrows, body, 0)

    def wait_all(slot):
        def body(r, carry):
            _row_gather_copy(y_hbm, 0, ybuf.at[slot], r, sems.at[slot]).wait()
            return carry
        lax.fori_loop(0, nrows, body, 0)

    slot = i % 2

    @pl.when(i == 0)
    def _():
        issue(0, 0)

    @pl.when(i + 1 < nsteps)
    def _():
        issue(i + 1, 1 - slot)

    wait_all(slot)
    acc = h_ref[...]
    for kk in range(TOP_K):
        acc = acc + ybuf[slot, kk * t:(kk + 1) * t, :]
    out_ref[...] = acc


def _combine(dest_km, h1, y_pad):
    n, d = h1.shape
    t = T_COMBINE
    grid_spec = pltpu.PrefetchScalarGridSpec(
        num_scalar_prefetch=1,
        grid=(n // t,),
        in_specs=[pl.BlockSpec((t, d), lambda i, *_: (i, 0)), pl.BlockSpec(memory_space=pl.ANY)],
        out_specs=pl.BlockSpec((t, d), lambda i, *_: (i, 0)),
        scratch_shapes=[pltpu.VMEM((2, TOP_K * t, d), F32), pltpu.SemaphoreType.DMA((2,))],
    )
    return pl.pallas_call(
        _combine_body,
        grid_spec=grid_spec,
        out_shape=jax.ShapeDtypeStruct((n, d), F32),
        compiler_params=_params("arbitrary"),
        name="combine",
    )(dest_km, h1, y_pad)


def _dispatch_tables(expert_id, gate, n_experts):
    n = expert_id.shape[0]
    m = n * TOP_K
    flat_e = expert_id.reshape(m)
    order = jnp.argsort(flat_e)
    sorted_e = flat_e[order]
    counts = jnp.bincount(flat_e, length=n_experts)
    padded = ((counts + MOE_BLOCK - 1) // MOE_BLOCK) * MOE_BLOCK
    start = jnp.cumsum(counts) - counts
    pad_end = jnp.cumsum(padded)
    pad_start = pad_end - padded
    dest_sorted = (pad_start[sorted_e] + (jnp.arange(m, dtype=jnp.int32) - start[sorted_e])).astype(jnp.int32)
    m_pad = m + n_experts * MOE_BLOCK
    n_blocks = m_pad // MOE_BLOCK
    tok_buf = jnp.zeros((m_pad,), jnp.int32).at[dest_sorted].set((order // TOP_K).astype(jnp.int32))
    w_buf = jnp.zeros((m_pad,), F32).at[dest_sorted].set(gate.reshape(m)[order])
    dest = jnp.zeros((m,), jnp.int32).at[order].set(dest_sorted)
    block_e = jnp.minimum(
        jnp.searchsorted(pad_end, jnp.arange(n_blocks, dtype=jnp.int32) * MOE_BLOCK, side="right"),
        n_experts - 1).astype(jnp.int32)
    n_used = (pad_end[-1] // MOE_BLOCK).astype(jnp.int32).reshape(1)
    return tok_buf, w_buf, dest, block_e, n_used


def _layer(x, positions, norm1_g, w_in, conv_w, conv_b, lru_wa, lru_ba, lru_wx, lru_bx, lru_lambda, w_rnn_o,
           q_norm_g, w_uq, kv_norm_g, w_ukv, qk_norm_q_g, qk_norm_k_g, w_mla_o, w_out, norm2_g,
           router_wg, router_bg, router_we, router_be, exp_w1, exp_w3, exp_w2):
    b, s, d = x.shape
    n = b * s
    d_rnn = conv_w.shape[-1]
    q_lora = q_norm_g.shape[-1]
    kv_lora = kv_norm_g.shape[-1]
    nh, half = N_HEADS, QK_ROPE // 2
    row = lambda a: a.reshape(1, -1).astype(F32)

    o_xr, o_yr, o_cq, o_ckv, o_kpe, o_ga, o_gb = np.cumsum([0, d_rnn, d_rnn, q_lora, kv_lora, QK_ROPE, d])
    w_kpe = w_in[:, o_kpe:o_kpe + QK_ROPE]
    w_kpe_rot = jnp.concatenate([-w_kpe[:, half:], w_kpe[:, :half]], axis=1)
    w_in_k = jnp.concatenate([w_in[:, :o_kpe], w_in[:, o_ga:], w_kpe, w_kpe_rot], axis=1).astype(BF16)
    widths = (d_rnn, d_rnn, q_lora, kv_lora, d, d, 2 * QK_ROPE)
    dtypes = (BF16, BF16, F32, F32, BF16, BF16, F32)

    x2 = x.reshape(n, d)
    xr, yr, cq, ckv, ga, gb, kpe2 = _in_proj(x2, row(norm1_g), w_in_k, widths, dtypes)

    w_gate = jnp.concatenate([lru_wa, lru_wx], axis=-1).astype(BF16)
    gated_a = _rglru(xr.reshape(b, s, d_rnn), yr.reshape(b, s, d_rnn), ga.reshape(b, s, d),
                     conv_w.astype(F32), row(conv_b), w_gate, row(lru_ba), row(lru_bx), row(lru_lambda),
                     w_rnn_o.astype(BF16))

    def rot_cols(w):
        return jnp.concatenate([jnp.zeros_like(w[..., :QK_NOPE]), -w[..., QK_NOPE + half:],
                                w[..., QK_NOPE:QK_NOPE + half]], axis=-1)

    wq = w_uq.reshape(q_lora, nh, QK_DIM).transpose(1, 0, 2)
    wkv = w_ukv.reshape(kv_lora, nh, QK_NOPE + V_DIM).transpose(1, 0, 2).astype(BF16)
    gq = row(qk_norm_q_g)
    gk = row(qk_norm_k_g)
    inv_freq = ROPE_THETA ** (-jnp.arange(0, QK_ROPE, 2, dtype=F32) / QK_ROPE)
    freq = jnp.concatenate([jnp.zeros((QK_NOPE,), F32), inv_freq, inv_freq]).reshape(1, QK_DIM)
    gk_r = gk[:, QK_NOPE:]
    gk_r_rot = jnp.concatenate([gk_r[:, half:], gk_r[:, :half]], axis=1)
    gq_rot = jnp.concatenate([jnp.zeros((1, QK_NOPE), F32), gq[:, QK_NOPE + half:], gq[:, QK_NOPE:QK_NOPE + half]],
                             axis=1)
    consts = (row(q_norm_g), row(kv_norm_g), wq.astype(BF16), rot_cols(wq).astype(BF16), wkv,
              gq, gq_rot, gk[:, :QK_NOPE], gk_r, gk_r_rot, freq)
    pos = positions.astype(F32).reshape(b, s, 1)
    q, k, v = _mla_proj(cq.reshape(b, s, q_lora), ckv.reshape(b, s, kv_lora), kpe2.reshape(b, s, 2 * QK_ROPE),
                        pos, consts)
    o = _attention(q, k, v)

    n_groups = router_wg.shape[-1]
    n_experts = router_we.shape[-1]
    per_group = n_experts // n_groups
    wr = jnp.concatenate([router_wg, router_we,
                          jnp.zeros((d, ROUTER_LANES - n_groups - n_experts), F32)], axis=1).astype(F32)
    wr_hi = wr.astype(BF16)
    wr_lo = (wr - wr_hi.astype(F32)).astype(BF16)
    rb = jnp.concatenate([router_bg, router_be,
                          jnp.zeros((ROUTER_LANES - n_groups - n_experts,), F32)]).reshape(1, ROUTER_LANES).astype(F32)
    h1, rec = _merge(o.reshape(n, d), gated_a.reshape(n, d), gb, x2, w_mla_o.astype(BF16), w_out.astype(BF16),
                     row(norm2_g), wr_hi, wr_lo, rb, n_groups, per_group)

    expert_id = rec[:, :TOP_K].astype(jnp.int32)
    gate = rec[:, TOP_K:2 * TOP_K]
    tok_buf, w_buf, dest, block_e, n_used = _dispatch_tables(expert_id, gate, n_experts)
    y_pad = _experts(tok_buf, block_e, n_used, h1, row(norm2_g), w_buf.reshape(-1, 1),
                     exp_w1.astype(BF16), exp_w3.astype(BF16), exp_w2.astype(BF16))
    t = T_COMBINE
    dest_km = dest.reshape(n // t, t, TOP_K).transpose(0, 2, 1).reshape(-1)
    out = _combine(dest_km, h1, y_pad)
    return out.reshape(b, s, d)


def kernel(x, positions, norm1_g, w_in, conv_w, conv_b, lru_wa, lru_ba, lru_wx, lru_bx, lru_lambda, w_rnn_o, q_norm_g, w_uq, kv_norm_g, w_ukv, qk_norm_q_g, qk_norm_k_g, w_mla_o, w_out, norm2_g, router_wg, router_bg, router_we, router_be, exp_w1, exp_w3, exp_w2):
    args = (norm1_g, w_in, conv_w, conv_b, lru_wa, lru_ba, lru_wx, lru_bx, lru_lambda, w_rnn_o, q_norm_g, w_uq,
            kv_norm_g, w_ukv, qk_norm_q_g, qk_norm_k_g, w_mla_o, w_out, norm2_g, router_wg, router_bg, router_we,
            router_be, exp_w1, exp_w3, exp_w2)
    depth = norm1_g.shape[0]
    h = x
    for l in range(depth):
        h = _layer(h, positions, *[a[l] for a in args])
    return h
```

```python
import functools
import math

import numpy as np
import jax
import jax.numpy as jnp
from jax import lax
from jax.experimental import pallas as pl
from jax.experimental.pallas import tpu as pltpu

F32 = jnp.float32
BF16 = jnp.bfloat16
EPS = 1e-6

N_HEADS = 8
QK_NOPE = 128
QK_ROPE = 64
QK_DIM = QK_NOPE + QK_ROPE
V_DIM = 128
ROPE_THETA = 10000.0
LRU_C = 8.0
TOP_K = 2
MOE_BLOCK = 128
ROUTER_LANES = 128

SUBLANES = 8
LANES = 128
VMEM_LIMIT_BYTES = 56 * 1024 * 1024

T_INPROJ = 256
T_RGLRU = 256
T_KV = 512
T_MLAPROJ = T_KV
T_Q = 1024
Q_CHAIN = 256
QK_AHEAD = 3
V_ROWS = V_DIM + 16
T_MERGE = 256
T_COMBINE = 256
ROW_CHUNKS = 8


def _params(*sem):
    return pltpu.CompilerParams(dimension_semantics=sem, vmem_limit_bytes=VMEM_LIMIT_BYTES)


def _sigmoid(z):
    return 1.0 / (1.0 + jnp.exp(-z))


def _const_spec(shape):
    n = len(shape)
    return pl.BlockSpec(shape, lambda *_: (0,) * n)


def _rows_to_natural(ref, first, rows):
    return jnp.concatenate([ref[pl.ds(first + c, rows, stride=ROW_CHUNKS), :] for c in range(ROW_CHUNKS)], axis=1)


def _natural_to_rows(ref, value):
    rows = value.shape[0]
    for c in range(ROW_CHUNKS):
        ref[pl.ds(c, rows, stride=ROW_CHUNKS), :] = value[:, c * LANES:(c + 1) * LANES]


def _inproj_body(x_ref, g_ref, w_ref, *out_refs):
    x = x_ref[...]
    u = x * lax.rsqrt(jnp.mean(x * x, axis=-1, keepdims=True) + EPS) * g_ref[...]
    u = u.astype(BF16)
    off = 0
    for ref in out_refs:
        n = ref.shape[-1]
        ref[...] = jnp.dot(u, w_ref[:, off:off + n], preferred_element_type=F32).astype(ref.dtype)
        off += n


def _in_proj(x2, g, w, widths, dtypes):
    n, d = x2.shape
    t = T_INPROJ
    return pl.pallas_call(
        _inproj_body,
        grid=(n // t,),
        in_specs=[pl.BlockSpec((t, d), lambda i: (i, 0)), _const_spec(g.shape), _const_spec(w.shape)],
        out_specs=[pl.BlockSpec((t, c), lambda i: (i, 0)) for c in widths],
        out_shape=[jax.ShapeDtypeStruct((n, c), dt) for c, dt in zip(widths, dtypes)],
        compiler_params=_params("parallel"),
        name="in_proj",
    )(x2, g, w)


def _rglru_body(xr_ref, yr_ref, ga_ref, cw_ref, cb_ref, wg_ref, ba_ref, bx_ref, lam_ref, wo_ref,
                out_ref, xbuf, a_s, b_s, h_s, hcar):
    t, c = xr_ref.shape
    nblk, bw, _ = wg_ref.shape

    @pl.when(pl.program_id(1) == 0)
    def _():
        xbuf[0:SUBLANES, :] = jnp.zeros((SUBLANES, c), F32)
        hcar[...] = jnp.zeros_like(hcar)

    xbuf[SUBLANES:, :] = xr_ref[...].astype(F32)
    lam = lam_ref[...]
    sp = jnp.maximum(-lam, 0.0) + jnp.log(1.0 + jnp.exp(-jnp.abs(lam)))
    rowmod = lax.broadcasted_iota(jnp.int32, (t, bw), 0) % SUBLANES

    for n in range(nblk):
        sl = slice(n * bw, (n + 1) * bw)
        xc = cb_ref[:, sl] + cw_ref[3:4, sl] * xbuf[SUBLANES:SUBLANES + t, sl]
        for s in (1, 2, 3):
            xc = xc + cw_ref[3 - s:4 - s, sl] * xbuf[SUBLANES - s:SUBLANES - s + t, sl]
        g = jnp.dot(xc.astype(BF16), wg_ref[n], preferred_element_type=F32)
        r = _sigmoid(g[:, :bw] + ba_ref[:, sl])
        i = _sigmoid(g[:, bw:] + bx_ref[:, sl])
        log_a = -LRU_C * r * sp[:, sl]
        a = jnp.exp(log_a)
        b = xc * i * jnp.sqrt(-jnp.tanh(log_a) * (a * a + 1.0))
        for d in (1, 2, 4):
            keep = rowmod >= d
            a_sh = pltpu.roll(a, d, 0)
            b_sh = pltpu.roll(b, d, 0)
            b = jnp.where(keep, a * b_sh + b, b)
            a = jnp.where(keep, a * a_sh, a)
        a_s[:, sl] = a
        b_s[:, sl] = b

    xbuf[0:SUBLANES, :] = xbuf[t:t + SUBLANES, :]

    def group(gi, hb):
        rows = pl.ds(pl.multiple_of(gi * SUBLANES, SUBLANES), SUBLANES)
        h8 = a_s[rows, :] * hb + b_s[rows, :]
        h_s[rows, :] = h8
        return jnp.broadcast_to(h8[SUBLANES - 1:SUBLANES, :], (SUBLANES, c))

    hcar[...] = lax.fori_loop(0, t // SUBLANES, group, hcar[...])

    y = yr_ref[...].astype(F32)
    gelu = 0.5 * y * (1.0 + jnp.tanh(math.sqrt(2.0 / math.pi) * (y + 0.044715 * (y * y * y))))
    hg = (h_s[...] * gelu).astype(BF16)
    br = jnp.dot(hg, wo_ref[...], preferred_element_type=F32)
    out_ref[...] = (_sigmoid(ga_ref[...].astype(F32)) * br).astype(out_ref.dtype)


def _rglru(xr, yr, ga, cw, cb, wg, ba, bx, lam, wo):
    b, s, c = xr.shape
    t = T_RGLRU
    tile = pl.BlockSpec((None, t, c), lambda bi, j: (bi, j, 0))
    consts = (cw, cb, wg, ba, bx, lam, wo)
    return pl.pallas_call(
        _rglru_body,
        grid=(b, s // t),
        in_specs=[tile, tile, tile] + [_const_spec(a.shape) for a in consts],
        out_specs=tile,
        out_shape=jax.ShapeDtypeStruct((b, s, c), BF16),
        scratch_shapes=[
            pltpu.VMEM((t + SUBLANES, c), F32),
            pltpu.VMEM((t, c), F32),
            pltpu.VMEM((t, c), F32),
            pltpu.VMEM((t, c), F32),
            pltpu.VMEM((SUBLANES, c), F32),
        ],
        compiler_params=_params("parallel", "arbitrary"),
        name="rglru",
    )(xr, yr, ga, *consts)


def _mlaproj_body(cq_ref, ckv_ref, kpe_ref, posc_ref, posr_ref, qg_ref, kvg_ref, wqt_ref, wqrt_ref, wk_ref, wvt_ref,
                  gqn_ref, gqr_ref, gqrr_ref, gkn_ref, gkr_ref, gkrr_ref, frow_ref, fcol_ref,
                  qt_ref, k_ref, vt_ref):
    nh = wqt_ref.shape[0]
    t = cq_ref.shape[0]

    def latent_norm(ref, g_ref):
        z = ref[...]
        return z * lax.rsqrt(jnp.mean(z * z, axis=-1, keepdims=True) + EPS) * g_ref[...]

    cqn_t = latent_norm(cq_ref, qg_ref).T.astype(BF16)
    ckvn = latent_norm(ckv_ref, kvg_ref)
    ckvn_t = ckvn.T.astype(BF16)
    ckvn = ckvn.astype(BF16)

    ang_t = fcol_ref[...] * posr_ref[...]
    gr_cos = gqr_ref[...] * jnp.cos(ang_t)
    grr_sin = gqrr_ref[...] * jnp.sin(ang_t)
    qscale = QK_DIM ** -0.5 * math.log2(math.e)
    for h in range(nh):
        qh = jnp.dot(wqt_ref[h], cqn_t, preferred_element_type=F32)
        qr = jnp.dot(wqrt_ref[h], cqn_t, preferred_element_type=F32)
        rstd = lax.rsqrt(jnp.mean(qh * qh, axis=0, keepdims=True) + EPS) * qscale
        qt_ref[h, :QK_NOPE, :] = (qh[:QK_NOPE] * gqn_ref[...] * rstd).astype(qt_ref.dtype)
        qt_ref[h, QK_NOPE:, :] = ((qh[QK_NOPE:] * gr_cos + qr * grr_sin) * rstd).astype(qt_ref.dtype)

    ang = posc_ref[...] * frow_ref[...]
    kpe = kpe_ref[:, :QK_ROPE]
    kpr = kpe_ref[:, QK_ROPE:]
    k_rot = kpe * gkr_ref[...] * jnp.cos(ang) + kpr * gkrr_ref[...] * jnp.sin(ang)
    ss_pe = jnp.sum(kpe * kpe, axis=-1, keepdims=True)
    ones_rows = (lax.broadcasted_iota(jnp.int32, (V_ROWS - V_DIM, t), 0) == 0).astype(vt_ref.dtype)
    for h in range(nh):
        kn = jnp.dot(ckvn, wk_ref[h], preferred_element_type=F32)
        rstd = lax.rsqrt((jnp.sum(kn * kn, axis=-1, keepdims=True) + ss_pe) * (1.0 / QK_DIM) + EPS)
        k_ref[h, :, :QK_NOPE] = (kn * gkn_ref[...] * rstd).astype(k_ref.dtype)
        k_ref[h, :, QK_NOPE:] = (k_rot * rstd).astype(k_ref.dtype)
        vt_ref[h, :V_DIM, :] = jnp.dot(wvt_ref[h], ckvn_t, preferred_element_type=F32).astype(vt_ref.dtype)
        vt_ref[h, V_DIM:, :] = ones_rows


def _mla_proj(cq, ckv, kpe2, pos_col, pos_row, consts):
    b, s, _ = cq.shape
    t = T_MLAPROJ
    nh = N_HEADS

    def tile(c):
        return pl.BlockSpec((None, t, c), lambda bi, j: (bi, j, 0))

    return pl.pallas_call(
        _mlaproj_body,
        grid=(b, s // t),
        in_specs=[tile(cq.shape[-1]), tile(ckv.shape[-1]), tile(kpe2.shape[-1]), tile(1),
                  pl.BlockSpec((None, 1, t), lambda bi, j: (bi, 0, j))]
        + [_const_spec(a.shape) for a in consts],
        out_specs=[pl.BlockSpec((None, nh, QK_DIM, t), lambda bi, j: (bi, 0, 0, j)),
                   pl.BlockSpec((None, nh, t, QK_DIM), lambda bi, j: (bi, 0, j, 0)),
                   pl.BlockSpec((None, nh, None, V_ROWS, t), lambda bi, j: (bi, 0, j, 0, 0))],
        out_shape=[jax.ShapeDtypeStruct((b, nh, QK_DIM, s), BF16),
                   jax.ShapeDtypeStruct((b, nh, s, QK_DIM), BF16),
                   jax.ShapeDtypeStruct((b, nh, s // t, V_ROWS, t), BF16)],
        compiler_params=_params("parallel", "parallel"),
        name="mla_proj",
    )(cq, ckv, kpe2, pos_col, pos_row, *consts)


def _attn_body(qt_ref, k_ref, vt_ref, o_ref, m_s, acc_s):
    tq = qt_ref.shape[1]
    tk = vt_ref.shape[-1]
    dv = o_ref.shape[-1]
    nchain = tq // Q_CHAIN
    per_tile = tq // tk
    nfull = pl.program_id(2) * per_tile
    m_s[...] = jnp.full(m_s.shape, -jnp.inf, F32)
    acc_s[...] = jnp.zeros(acc_s.shape, F32)

    def scores(j, c, key_minus_query):
        kb = k_ref[pl.ds(pl.multiple_of(j * tk, tk), tk), :]
        s = jnp.dot(kb, qt_ref[:, c * Q_CHAIN:(c + 1) * Q_CHAIN], preferred_element_type=F32)
        if key_minus_query is not None:
            krow = lax.broadcasted_iota(jnp.int32, s.shape, 0) + key_minus_query
            qcol = lax.broadcasted_iota(jnp.int32, s.shape, 1)
            s = jnp.where(krow <= qcol, s, -jnp.inf)
        return s

    def softmax_pv(j, c, s):
        cs = slice(c * Q_CHAIN, (c + 1) * Q_CHAIN)
        r = s
        while r.shape[0] > SUBLANES:
            half = r.shape[0] // 2
            r = jnp.maximum(r[:half], r[half:])
        m_old = m_s[:, cs]
        m_new = jnp.maximum(m_old, jnp.max(r, axis=0, keepdims=True))
        alpha = jnp.exp2(m_old - m_new)
        p = jnp.exp2(s - m_new).astype(BF16)
        m_s[:, cs] = m_new
        acc_s[:, cs] = alpha * acc_s[:, cs] + jnp.dot(vt_ref[j], p, preferred_element_type=F32)

    def run(todo):
        pending = {i: scores(*todo[i]) for i in range(min(QK_AHEAD, len(todo)))}
        for i, (j, c, _) in enumerate(todo):
            if i + QK_AHEAD < len(todo):
                pending[i + QK_AHEAD] = scores(*todo[i + QK_AHEAD])
            softmax_pv(j, c, pending.pop(i))

    def full_blocks(jj, carry):
        run([(jj * per_tile + u, c, None) for u in range(per_tile) for c in range(nchain)])
        return carry

    lax.fori_loop(0, nfull // per_tile, full_blocks, 0)

    diagonal = []
    for u in range(per_tile):
        for c in range(nchain):
            k0, q0 = u * tk, c * Q_CHAIN
            if k0 > q0 + Q_CHAIN - 1:
                continue
            diagonal.append((nfull + u, c, None if k0 + tk - 1 <= q0 else k0 - q0))
    run(diagonal)

    out_t = acc_s[:dv, :] / acc_s[dv:dv + 1, :]
    o_ref[...] = out_t.T.astype(o_ref.dtype)


def _attention(qt, k, vt):
    b, nh, dqk, s = qt.shape
    nkb, vrows, tk = vt.shape[2:]
    tq = T_Q
    return pl.pallas_call(
        _attn_body,
        grid=(b, nh, s // tq),
        in_specs=[pl.BlockSpec((None, None, dqk, tq), lambda bi, h, i: (bi, h, 0, i)),
                  pl.BlockSpec((None, None, s, dqk), lambda bi, h, i: (bi, h, 0, 0)),
                  pl.BlockSpec((None, None, nkb, vrows, tk), lambda bi, h, i: (bi, h, 0, 0, 0))],
        out_specs=pl.BlockSpec((None, tq, V_DIM), lambda bi, h, i: (bi, i, h)),
        out_shape=jax.ShapeDtypeStruct((b, s, nh * V_DIM), BF16),
        scratch_shapes=[pltpu.VMEM((1, tq), F32), pltpu.VMEM((vrows, tq), F32)],
        compiler_params=_params("parallel", "parallel", "arbitrary"),
        name="attention",
    )(qt, k, vt)


def _merge_body(o_ref, ga_ref, gb_ref, x_ref, wmo_ref, wout_ref, g2_ref, wrh_ref, wrl_ref, rb_ref,
                hrow_ref, rec_ref, cnt_ref, cnt_s, *, n_groups, per_group):
    @pl.when(pl.program_id(0) == 0)
    def _():
        cnt_s[...] = jnp.zeros_like(cnt_s)

    br_b = jnp.dot(o_ref[...], wmo_ref[...], preferred_element_type=F32)
    merged = ga_ref[...].astype(F32) + _sigmoid(gb_ref[...].astype(F32)) * br_b
    h = x_ref[...] + jnp.dot(merged.astype(BF16), wout_ref[...], preferred_element_type=F32)
    _natural_to_rows(hrow_ref, h)

    u = h * lax.rsqrt(jnp.mean(h * h, axis=-1, keepdims=True) + EPS) * g2_ref[...]
    u_hi = u.astype(BF16)
    u_lo = (u - u_hi.astype(F32)).astype(BF16)
    logits = (jnp.dot(u_hi, wrh_ref[...], preferred_element_type=F32)
              + jnp.dot(u_lo, wrh_ref[...], preferred_element_type=F32)
              + jnp.dot(u_hi, wrl_ref[...], preferred_element_type=F32)) + rb_ref[...]

    t, nl = logits.shape
    lane = lax.broadcasted_iota(jnp.int32, (t, nl), 1)
    neg = -jnp.inf

    def first_argmax(z, zmax):
        return jnp.min(jnp.where(z == zmax, lane, nl), axis=-1, keepdims=True)

    zg = jnp.where(lane < n_groups, logits, neg)
    mg = jnp.max(zg, axis=-1, keepdims=True)
    g_sel = first_argmax(zg, mg)
    g_prob = 1.0 / jnp.sum(jnp.exp(zg - mg), axis=-1, keepdims=True)
    lo = n_groups + g_sel * per_group
    ze = jnp.where((lane >= lo) & (lane < lo + per_group), logits, neg)
    m1 = jnp.max(ze, axis=-1, keepdims=True)
    i1 = first_argmax(ze, m1)
    ze2 = jnp.where(lane == i1, neg, ze)
    m2 = jnp.max(ze2, axis=-1, keepdims=True)
    i2 = first_argmax(ze2, m2)
    e21 = jnp.exp(m2 - m1)
    w1 = g_prob / (1.0 + e21)
    w2 = w1 * e21

    e1, e2 = i1 - n_groups, i2 - n_groups
    hot1 = lane == e1
    hot2 = lane == e2
    hot = jnp.where(hot1 | hot2, 1.0, 0.0).astype(BF16)
    tri = (lax.broadcasted_iota(jnp.int32, (t, t), 1) < lax.broadcasted_iota(jnp.int32, (t, t), 0)).astype(BF16)
    before = jnp.dot(tri, hot, preferred_element_type=F32) + cnt_s[0:1, :]
    rank1 = jnp.sum(jnp.where(hot1, before, 0.0), axis=-1, keepdims=True)
    rank2 = jnp.sum(jnp.where(hot2, before, 0.0), axis=-1, keepdims=True)
    cnt_s[...] = cnt_s[...] + jnp.sum(hot.astype(F32), axis=0, keepdims=True)
    cnt_ref[...] = cnt_s[...]

    rec = jnp.where(lane == 0, e1.astype(F32), 0.0)
    for col, val in ((1, e2.astype(F32)), (2, w1), (3, w2), (4, rank1), (5, rank2)):
        rec = jnp.where(lane == col, val, rec)
    rec_ref[...] = rec


def _merge(o2, ga2, gb2, x2, wmo, wout, g2, wrh, wrl, rb, n_groups, per_group):
    n, d = x2.shape
    t = T_MERGE
    consts = (wmo, wout, g2, wrh, wrl, rb)
    tile = pl.BlockSpec((t, d), lambda i: (i, 0))
    return pl.pallas_call(
        functools.partial(_merge_body, n_groups=n_groups, per_group=per_group),
        grid=(n // t,),
        in_specs=[tile, tile, tile, tile] + [_const_spec(a.shape) for a in consts],
        out_specs=[pl.BlockSpec((t * ROW_CHUNKS, LANES), lambda i: (i, 0)),
                   pl.BlockSpec((t, ROUTER_LANES), lambda i: (i, 0)),
                   pl.BlockSpec((SUBLANES, ROUTER_LANES), lambda i: (0, 0))],
        out_shape=[jax.ShapeDtypeStruct((n * ROW_CHUNKS, LANES), F32),
                   jax.ShapeDtypeStruct((n, ROUTER_LANES), F32),
                   jax.ShapeDtypeStruct((SUBLANES, ROUTER_LANES), F32)],
        scratch_shapes=[pltpu.VMEM((SUBLANES, ROUTER_LANES), F32)],
        compiler_params=_params("arbitrary"),
        name="merge_router",
    )(o2, ga2, gb2, x2, *consts)


def _slab_copy(src_hbm, row, dst, dst_row, sem):
    return pltpu.make_async_copy(
        src_hbm.at[pl.ds(pl.multiple_of(row * ROW_CHUNKS, ROW_CHUNKS), ROW_CHUNKS), :],
        dst.at[pl.ds(pl.multiple_of(dst_row * ROW_CHUNKS, ROW_CHUNKS), ROW_CHUNKS), :], sem)


def _issue_rows(idx_ref, first, src_hbm, dst, sem, nrows):
    def body(r, carry):
        _slab_copy(src_hbm, idx_ref[first + r], dst, r, sem).start()
        return carry
    lax.fori_loop(0, nrows, body, 0)


def _wait_rows(src_hbm, dst, sem, nrows):
    def body(r, carry):
        _slab_copy(src_hbm, 0, dst, r, sem).wait()
        return carry
    lax.fori_loop(0, nrows, body, 0)


def _experts_body(tok_ref, be_ref, nb_ref, h_hbm, g2_ref, w1_ref, w3_ref, w2_ref, y_ref, xbuf, sems):
    blk = pl.program_id(0)
    nblk = pl.num_programs(0)
    rows = MOE_BLOCK
    n_used = nb_ref[0]
    slot = blk % 2

    @pl.when((blk == 0) & (n_used > 0))
    def _():
        _issue_rows(tok_ref, 0, h_hbm, xbuf.at[0], sems.at[0], rows)

    @pl.when(blk + 1 < jnp.minimum(n_used, nblk))
    def _():
        _issue_rows(tok_ref, (blk + 1) * rows, h_hbm, xbuf.at[1 - slot], sems.at[1 - slot], rows)

    @pl.when(blk < n_used)
    def _():
        _wait_rows(h_hbm, xbuf.at[slot], sems.at[slot], rows)
        x = _rows_to_natural(xbuf.at[slot], 0, rows)
        u = (x * lax.rsqrt(jnp.mean(x * x, axis=-1, keepdims=True) + EPS) * g2_ref[...]).astype(BF16)
        a = jnp.dot(u, w1_ref[...], preferred_element_type=F32)
        c = jnp.dot(u, w3_ref[...], preferred_element_type=F32)
        hmid = (a * _sigmoid(a) * c).astype(BF16)
        _natural_to_rows(y_ref, jnp.dot(hmid, w2_ref[...], preferred_element_type=F32))

    @pl.when(blk >= n_used)
    def _():
        y_ref[...] = jnp.zeros_like(y_ref)


def _experts(tok_buf, block_e, n_used, h_rows, g2, w1, w3, w2):
    d = g2.shape[-1]
    m_pad = tok_buf.shape[0]
    rows = MOE_BLOCK
    nblk = m_pad // rows
    de = w1.shape[-1]
    grid_spec = pltpu.PrefetchScalarGridSpec(
        num_scalar_prefetch=3,
        grid=(nblk,),
        in_specs=[
            pl.BlockSpec(memory_space=pl.ANY),
            pl.BlockSpec(g2.shape, lambda i, *_: (0, 0)),
            pl.BlockSpec((None, d, de), lambda i, tok, be, nb: (be[i], 0, 0)),
            pl.BlockSpec((None, d, de), lambda i, tok, be, nb: (be[i], 0, 0)),
            pl.BlockSpec((None, de, d), lambda i, tok, be, nb: (be[i], 0, 0)),
        ],
        out_specs=pl.BlockSpec((rows * ROW_CHUNKS, LANES), lambda i, *_: (i, 0)),
        scratch_shapes=[pltpu.VMEM((2, rows * ROW_CHUNKS, LANES), F32), pltpu.SemaphoreType.DMA((2,))],
    )
    return pl.pallas_call(
        _experts_body,
        grid_spec=grid_spec,
        out_shape=jax.ShapeDtypeStruct((m_pad * ROW_CHUNKS, LANES), F32),
        compiler_params=_params("arbitrary"),
        name="experts",
    )(tok_buf, block_e, n_used, h_rows, g2, w1, w3, w2)


def _combine_body(dest_ref, hrow_ref, rec_ref, y_hbm, out_ref, ybuf, sems):
    i = pl.program_id(0)
    nsteps = pl.num_programs(0)
    t = rec_ref.shape[0]
    nrows = TOP_K * t
    slot = i % 2

    @pl.when(i == 0)
    def _():
        _issue_rows(dest_ref, 0, y_hbm, ybuf.at[0], sems.at[0], nrows)

    @pl.when(i + 1 < nsteps)
    def _():
        _issue_rows(dest_ref, (i + 1) * nrows, y_hbm, ybuf.at[1 - slot], sems.at[1 - slot], nrows)

    _wait_rows(y_hbm, ybuf.at[slot], sems.at[slot], nrows)
    acc = _rows_to_natural(hrow_ref, 0, t)
    for kk in range(TOP_K):
        gate = rec_ref[:, TOP_K + kk:TOP_K + kk + 1]
        acc = acc + gate * _rows_to_natural(ybuf.at[slot], kk * t * ROW_CHUNKS, t)
    out_ref[...] = acc


def _combine(dest_km, h_rows, rec, y_rows):
    n = rec.shape[0]
    d = ROW_CHUNKS * LANES
    t = T_COMBINE
    grid_spec = pltpu.PrefetchScalarGridSpec(
        num_scalar_prefetch=1,
        grid=(n // t,),
        in_specs=[pl.BlockSpec((t * ROW_CHUNKS, LANES), lambda i, *_: (i, 0)),
                  pl.BlockSpec((t, ROUTER_LANES), lambda i, *_: (i, 0)),
                  pl.BlockSpec(memory_space=pl.ANY)],
        out_specs=pl.BlockSpec((t, d), lambda i, *_: (i, 0)),
        scratch_shapes=[pltpu.VMEM((2, TOP_K * t * ROW_CHUNKS, LANES), F32), pltpu.SemaphoreType.DMA((2,))],
    )
    return pl.pallas_call(
        _combine_body,
        grid_spec=grid_spec,
        out_shape=jax.ShapeDtypeStruct((n, d), F32),
        compiler_params=_params("arbitrary"),
        name="combine",
    )(dest_km, h_rows, rec, y_rows)


def _dispatch_tables(expert_id, rank, counts):
    n_experts = counts.shape[0]
    n = expert_id.shape[0]
    m = n * TOP_K
    m_pad = m + n_experts * MOE_BLOCK
    n_blocks = m_pad // MOE_BLOCK
    flat_e = expert_id.reshape(m)
    padded = ((counts + MOE_BLOCK - 1) // MOE_BLOCK) * MOE_BLOCK
    start = jnp.cumsum(counts) - counts
    pad_end = jnp.cumsum(padded)
    pad_start = pad_end - padded
    block_first = jnp.arange(n_blocks, dtype=jnp.int32) * MOE_BLOCK
    block_e = jnp.minimum(jnp.sum(pad_end[None, :] <= block_first[:, None], axis=1), n_experts - 1).astype(jnp.int32)
    n_used = (pad_end[-1] // MOE_BLOCK).astype(jnp.int32).reshape(1)
    order = jnp.argsort(flat_e, stable=True).astype(jnp.int32)
    off = (block_first - pad_start[block_e])[:, None] + jnp.arange(MOE_BLOCK, dtype=jnp.int32)[None, :]
    valid = off < counts[block_e][:, None]
    src = jnp.clip(start[block_e][:, None] + off, 0, m - 1)
    tok_buf = jnp.where(valid, order[src] // TOP_K, 0).reshape(m_pad).astype(jnp.int32)
    dest = (pad_start[flat_e] + rank.reshape(m)).astype(jnp.int32)
    return tok_buf, dest, block_e, n_used


def _layer(x, positions, norm1_g, w_in, conv_w, conv_b, lru_wa, lru_ba, lru_wx, lru_bx, lru_lambda, w_rnn_o,
           q_norm_g, w_uq, kv_norm_g, w_ukv, qk_norm_q_g, qk_norm_k_g, w_mla_o, w_out, norm2_g,
           router_wg, router_bg, router_we, router_be, exp_w1, exp_w3, exp_w2):
    b, s, d = x.shape
    n = b * s
    d_rnn = conv_w.shape[-1]
    q_lora = q_norm_g.shape[-1]
    kv_lora = kv_norm_g.shape[-1]
    nh, half = N_HEADS, QK_ROPE // 2
    row = lambda a: a.reshape(1, -1).astype(F32)
    col = lambda a: a.reshape(-1, 1).astype(F32)

    o_xr, o_yr, o_cq, o_ckv, o_kpe, o_ga, o_gb = np.cumsum([0, d_rnn, d_rnn, q_lora, kv_lora, QK_ROPE, d])
    w_kpe = w_in[:, o_kpe:o_kpe + QK_ROPE]
    w_kpe_rot = jnp.concatenate([-w_kpe[:, half:], w_kpe[:, :half]], axis=1)
    w_in_k = jnp.concatenate([w_in[:, :o_kpe], w_in[:, o_ga:], w_kpe, w_kpe_rot], axis=1).astype(BF16)
    widths = (d_rnn, d_rnn, q_lora, kv_lora, d, d, 2 * QK_ROPE)
    dtypes = (BF16, BF16, F32, F32, BF16, BF16, F32)

    x2 = x.reshape(n, d)
    xr, yr, cq, ckv, ga, gb, kpe2 = _in_proj(x2, row(norm1_g), w_in_k, widths, dtypes)

    w_gate = jnp.concatenate([lru_wa, lru_wx], axis=-1).astype(BF16)
    gated_a = _rglru(xr.reshape(b, s, d_rnn), yr.reshape(b, s, d_rnn), ga.reshape(b, s, d),
                     conv_w.astype(F32), row(conv_b), w_gate, row(lru_ba), row(lru_bx), row(lru_lambda),
                     w_rnn_o.astype(BF16))

    def rot(v):
        return jnp.concatenate([-v[..., half:], v[..., :half]], axis=-1)

    def swap(v):
        return jnp.concatenate([v[..., half:], v[..., :half]], axis=-1)

    wq = w_uq.reshape(q_lora, nh, QK_DIM).transpose(1, 0, 2)
    wq_t = wq.transpose(0, 2, 1).astype(BF16)
    wqr_t = rot(wq[..., QK_NOPE:]).transpose(0, 2, 1).astype(BF16)
    wkv = w_ukv.reshape(kv_lora, nh, QK_NOPE + V_DIM).transpose(1, 0, 2)
    wk = wkv[..., :QK_NOPE].astype(BF16)
    wv_t = wkv[..., QK_NOPE:].transpose(0, 2, 1).astype(BF16)
    gq = qk_norm_q_g.astype(F32)
    gk = qk_norm_k_g.astype(F32)
    inv_freq = ROPE_THETA ** (-jnp.arange(0, QK_ROPE, 2, dtype=F32) / QK_ROPE)
    freq = jnp.concatenate([inv_freq, inv_freq])
    consts = (row(q_norm_g), row(kv_norm_g), wq_t, wqr_t, wk, wv_t,
              col(gq[:QK_NOPE]), col(gq[QK_NOPE:]), col(swap(gq[QK_NOPE:])),
              row(gk[:QK_NOPE]), row(gk[QK_NOPE:]), row(swap(gk[QK_NOPE:])), row(freq), col(freq))
    pos = positions.astype(F32)
    qt, k, vt = _mla_proj(cq.reshape(b, s, q_lora), ckv.reshape(b, s, kv_lora), kpe2.reshape(b, s, 2 * QK_ROPE),
                          pos.reshape(b, s, 1), pos.reshape(b, 1, s), consts)
    o = _attention(qt, k, vt)

    n_groups = router_wg.shape[-1]
    n_experts = router_we.shape[-1]
    per_group = n_experts // n_groups
    wr = jnp.concatenate([router_wg, router_we,
                          jnp.zeros((d, ROUTER_LANES - n_groups - n_experts), F32)], axis=1).astype(F32)
    wr_hi = wr.astype(BF16)
    wr_lo = (wr - wr_hi.astype(F32)).astype(BF16)
    rb = jnp.concatenate([router_bg, router_be,
                          jnp.zeros((ROUTER_LANES - n_groups - n_experts,), F32)]).reshape(1, ROUTER_LANES).astype(F32)
    h_rows, rec, cnt = _merge(o.reshape(n, d), gated_a.reshape(n, d), gb, x2, w_mla_o.astype(BF16),
                              w_out.astype(BF16), row(norm2_g), wr_hi, wr_lo, rb, n_groups, per_group)

    expert_id = rec[:, :TOP_K].astype(jnp.int32)
    rank = rec[:, 2 * TOP_K:3 * TOP_K].astype(jnp.int32)
    counts = cnt[0, :n_experts].astype(jnp.int32)
    tok_buf, dest, block_e, n_used = _dispatch_tables(expert_id, rank, counts)
    y_rows = _experts(tok_buf, block_e, n_used, h_rows, row(norm2_g),
                      exp_w1.astype(BF16), exp_w3.astype(BF16), exp_w2.astype(BF16))
    t = T_COMBINE
    dest_km = dest.reshape(n // t, t, TOP_K).transpose(0, 2, 1).reshape(-1)
    out = _combine(dest_km, h_rows, rec, y_rows)
    return out.reshape(b, s, d)


def kernel(x, positions, norm1_g, w_in, conv_w, conv_b, lru_wa, lru_ba, lru_wx, lru_bx, lru_lambda, w_rnn_o, q_norm_g, w_uq, kv_norm_g, w_ukv, qk_norm_q_g, qk_norm_k_g, w_mla_o, w_out, norm2_g, router_wg, router_bg, router_we, router_be, exp_w1, exp_w3, exp_w2):
    args = (norm1_g, w_in, conv_w, conv_b, lru_wa, lru_ba, lru_wx, lru_bx, lru_lambda, w_rnn_o, q_norm_g, w_uq,
            kv_norm_g, w_ukv, qk_norm_q_g, qk_norm_k_g, w_mla_o, w_out, norm2_g, router_wg, router_bg, router_we,
            router_be, exp_w1, exp_w3, exp_w2)
    depth = norm1_g.shape[0]
    h = x
    for l in range(depth):
        h = _layer(h, positions, *[a[l] for a in args])
    return h
```

```python
import functools
import math

import numpy as np
import jax
import jax.numpy as jnp
from jax import lax
from jax.experimental import pallas as pl
from jax.experimental.pallas import tpu as pltpu

F32 = jnp.float32
BF16 = jnp.bfloat16
EPS = 1e-6

N_HEADS = 8
QK_NOPE = 128
QK_ROPE = 64
QK_DIM = QK_NOPE + QK_ROPE
V_DIM = 128
ROPE_THETA = 10000.0
LRU_C = 8.0
TOP_K = 2
MOE_BLOCK = 128
ROUTER_LANES = 128

SUBLANES = 8
LANES = 128
VMEM_LIMIT_BYTES = 56 * 1024 * 1024

T_INPROJ = 256
T_RGLRU = 256
T_KV = 512
T_MLAPROJ = T_KV
T_Q = 1024
Q_CHAIN = 256
QK_AHEAD = 3
V_ROWS = V_DIM + 16
T_MERGE = 256
T_COMBINE = 256
ROW_CHUNKS = 8


def _params(*sem):
    return pltpu.CompilerParams(dimension_semantics=sem, vmem_limit_bytes=VMEM_LIMIT_BYTES)


def _sigmoid(z):
    return 0.5 * jnp.tanh(0.5 * z) + 0.5


def _const_spec(shape):
    n = len(shape)
    return pl.BlockSpec(shape, lambda *_: (0,) * n)


def _rows_to_natural(ref, first, rows):
    return jnp.concatenate([ref[pl.ds(first + c, rows, stride=ROW_CHUNKS), :] for c in range(ROW_CHUNKS)], axis=1)


def _natural_to_rows(ref, value):
    rows = value.shape[0]
    for c in range(ROW_CHUNKS):
        ref[pl.ds(c, rows, stride=ROW_CHUNKS), :] = value[:, c * LANES:(c + 1) * LANES]


def _inproj_body(x_ref, g_ref, w_ref, *out_refs):
    x = x_ref[...]
    u = x * lax.rsqrt(jnp.mean(x * x, axis=-1, keepdims=True) + EPS) * g_ref[...]
    u = u.astype(BF16)
    off = 0
    for ref in out_refs:
        n = ref.shape[-1]
        ref[...] = jnp.dot(u, w_ref[:, off:off + n], preferred_element_type=F32).astype(ref.dtype)
        off += n


def _in_proj(x2, g, w, widths, dtypes):
    n, d = x2.shape
    t = T_INPROJ
    return pl.pallas_call(
        _inproj_body,
        grid=(n // t,),
        in_specs=[pl.BlockSpec((t, d), lambda i: (i, 0)), _const_spec(g.shape), _const_spec(w.shape)],
        out_specs=[pl.BlockSpec((t, c), lambda i: (i, 0)) for c in widths],
        out_shape=[jax.ShapeDtypeStruct((n, c), dt) for c, dt in zip(widths, dtypes)],
        compiler_params=_params("parallel"),
        name="in_proj",
    )(x2, g, w)


def _rglru_body(xr_ref, yr_ref, ga_ref, cw_ref, cb_ref, wg_ref, ba_ref, bx_ref, lam_ref, wo_ref,
                out_ref, xbuf, a_s, b_s, h_s, hcar):
    t, c = xr_ref.shape
    nblk, bw, _ = wg_ref.shape

    @pl.when(pl.program_id(1) == 0)
    def _():
        xbuf[0:SUBLANES, :] = jnp.zeros((SUBLANES, c), F32)
        hcar[...] = jnp.zeros_like(hcar)

    xbuf[SUBLANES:, :] = xr_ref[...].astype(F32)
    lam = lam_ref[...]
    sp = jnp.maximum(-lam, 0.0) + jnp.log(1.0 + jnp.exp(-jnp.abs(lam)))
    rowmod = lax.broadcasted_iota(jnp.int32, (t, bw), 0) % SUBLANES

    for n in range(nblk):
        sl = slice(n * bw, (n + 1) * bw)
        xc = cb_ref[:, sl] + cw_ref[3:4, sl] * xbuf[SUBLANES:SUBLANES + t, sl]
        for s in (1, 2, 3):
            xc = xc + cw_ref[3 - s:4 - s, sl] * xbuf[SUBLANES - s:SUBLANES - s + t, sl]
        g = jnp.dot(xc.astype(BF16), wg_ref[n], preferred_element_type=F32)
        r = _sigmoid(g[:, :bw] + ba_ref[:, sl])
        i = _sigmoid(g[:, bw:] + bx_ref[:, sl])
        log_a = -LRU_C * r * sp[:, sl]
        a = jnp.exp(log_a)
        b = xc * i * jnp.sqrt(-jnp.tanh(log_a) * (a * a + 1.0))
        for d in (1, 2, 4):
            keep = rowmod >= d
            a_sh = pltpu.roll(a, d, 0)
            b_sh = pltpu.roll(b, d, 0)
            b = jnp.where(keep, a * b_sh + b, b)
            a = jnp.where(keep, a * a_sh, a)
        a_s[:, sl] = a
        b_s[:, sl] = b

    xbuf[0:SUBLANES, :] = xbuf[t:t + SUBLANES, :]

    def group(gi, hb):
        rows = pl.ds(pl.multiple_of(gi * SUBLANES, SUBLANES), SUBLANES)
        h8 = a_s[rows, :] * hb + b_s[rows, :]
        h_s[rows, :] = h8
        return jnp.broadcast_to(h8[SUBLANES - 1:SUBLANES, :], (SUBLANES, c))

    hcar[...] = lax.fori_loop(0, t // SUBLANES, group, hcar[...])

    y = yr_ref[...].astype(F32)
    gelu = 0.5 * y * (1.0 + jnp.tanh(math.sqrt(2.0 / math.pi) * (y + 0.044715 * (y * y * y))))
    hg = (h_s[...] * gelu).astype(BF16)
    br = jnp.dot(hg, wo_ref[...], preferred_element_type=F32)
    out_ref[...] = (_sigmoid(ga_ref[...].astype(F32)) * br).astype(out_ref.dtype)


def _rglru(xr, yr, ga, cw, cb, wg, ba, bx, lam, wo):
    b, s, c = xr.shape
    t = T_RGLRU
    tile = pl.BlockSpec((None, t, c), lambda bi, j: (bi, j, 0))
    consts = (cw, cb, wg, ba, bx, lam, wo)
    return pl.pallas_call(
        _rglru_body,
        grid=(b, s // t),
        in_specs=[tile, tile, tile] + [_const_spec(a.shape) for a in consts],
        out_specs=tile,
        out_shape=jax.ShapeDtypeStruct((b, s, c), BF16),
        scratch_shapes=[
            pltpu.VMEM((t + SUBLANES, c), F32),
            pltpu.VMEM((t, c), F32),
            pltpu.VMEM((t, c), F32),
            pltpu.VMEM((t, c), F32),
            pltpu.VMEM((SUBLANES, c), F32),
        ],
        compiler_params=_params("parallel", "arbitrary"),
        name="rglru",
    )(xr, yr, ga, *consts)


def _mlaproj_body(cq_ref, ckv_ref, kpe_ref, posc_ref, posr_ref, qg_ref, kvg_ref, wqt_ref, wqrt_ref, wk_ref, wvt_ref,
                  gqn_ref, gqr_ref, gqrr_ref, gkn_ref, gkr_ref, gkrr_ref, frow_ref, fcol_ref,
                  qt_ref, k_ref, vt_ref):
    nh = wqt_ref.shape[0]
    t = cq_ref.shape[0]

    def latent_norm(ref, g_ref):
        z = ref[...]
        return z * lax.rsqrt(jnp.mean(z * z, axis=-1, keepdims=True) + EPS) * g_ref[...]

    cqn_t = latent_norm(cq_ref, qg_ref).T.astype(BF16)
    ckvn = latent_norm(ckv_ref, kvg_ref)
    ckvn_t = ckvn.T.astype(BF16)
    ckvn = ckvn.astype(BF16)

    ang_t = fcol_ref[...] * posr_ref[...]
    gr_cos = gqr_ref[...] * jnp.cos(ang_t)
    grr_sin = gqrr_ref[...] * jnp.sin(ang_t)
    qscale = QK_DIM ** -0.5 * math.log2(math.e)
    for h in range(nh):
        qh = jnp.dot(wqt_ref[h], cqn_t, preferred_element_type=F32)
        qr = jnp.dot(wqrt_ref[h], cqn_t, preferred_element_type=F32)
        rstd = lax.rsqrt(jnp.mean(qh * qh, axis=0, keepdims=True) + EPS) * qscale
        qt_ref[h, :QK_NOPE, :] = (qh[:QK_NOPE] * gqn_ref[...] * rstd).astype(qt_ref.dtype)
        qt_ref[h, QK_NOPE:, :] = ((qh[QK_NOPE:] * gr_cos + qr * grr_sin) * rstd).astype(qt_ref.dtype)

    ang = posc_ref[...] * frow_ref[...]
    kpe = kpe_ref[:, :QK_ROPE]
    kpr = kpe_ref[:, QK_ROPE:]
    k_rot = kpe * gkr_ref[...] * jnp.cos(ang) + kpr * gkrr_ref[...] * jnp.sin(ang)
    ss_pe = jnp.sum(kpe * kpe, axis=-1, keepdims=True)
    ones_rows = (lax.broadcasted_iota(jnp.int32, (V_ROWS - V_DIM, t), 0) == 0).astype(vt_ref.dtype)
    for h in range(nh):
        kn = jnp.dot(ckvn, wk_ref[h], preferred_element_type=F32)
        rstd = lax.rsqrt((jnp.sum(kn * kn, axis=-1, keepdims=True) + ss_pe) * (1.0 / QK_DIM) + EPS)
        k_ref[h, :, :QK_NOPE] = (kn * gkn_ref[...] * rstd).astype(k_ref.dtype)
        k_ref[h, :, QK_NOPE:] = (k_rot * rstd).astype(k_ref.dtype)
        vt_ref[h, :V_DIM, :] = jnp.dot(wvt_ref[h], ckvn_t, preferred_element_type=F32).astype(vt_ref.dtype)
        vt_ref[h, V_DIM:, :] = ones_rows


def _mla_proj(cq, ckv, kpe2, pos_col, pos_row, consts):
    b, s, _ = cq.shape
    t = T_MLAPROJ
    nh = N_HEADS

    def tile(c):
        return pl.BlockSpec((None, t, c), lambda bi, j: (bi, j, 0))

    return pl.pallas_call(
        _mlaproj_body,
        grid=(b, s // t),
        in_specs=[tile(cq.shape[-1]), tile(ckv.shape[-1]), tile(kpe2.shape[-1]), tile(1),
                  pl.BlockSpec((None, 1, t), lambda bi, j: (bi, 0, j))]
        + [_const_spec(a.shape) for a in consts],
        out_specs=[pl.BlockSpec((None, nh, QK_DIM, t), lambda bi, j: (bi, 0, 0, j)),
                   pl.BlockSpec((None, nh, t, QK_DIM), lambda bi, j: (bi, 0, j, 0)),
                   pl.BlockSpec((None, nh, None, V_ROWS, t), lambda bi, j: (bi, 0, j, 0, 0))],
        out_shape=[jax.ShapeDtypeStruct((b, nh, QK_DIM, s), BF16),
                   jax.ShapeDtypeStruct((b, nh, s, QK_DIM), BF16),
                   jax.ShapeDtypeStruct((b, nh, s // t, V_ROWS, t), BF16)],
        compiler_params=_params("parallel", "parallel"),
        name="mla_proj",
    )(cq, ckv, kpe2, pos_col, pos_row, *consts)


def _attn_body(qt_ref, k_ref, vt_ref, o_ref, m_s, acc_s):
    tq = qt_ref.shape[1]
    tk = vt_ref.shape[-1]
    dv = o_ref.shape[-1]
    nchain = tq // Q_CHAIN
    per_tile = tq // tk
    nfull = pl.program_id(2) * per_tile
    m_s[...] = jnp.full(m_s.shape, -jnp.inf, F32)
    acc_s[...] = jnp.zeros(acc_s.shape, F32)

    def scores(j, c, key_minus_query):
        kb = k_ref[pl.ds(pl.multiple_of(j * tk, tk), tk), :]
        s = jnp.dot(kb, qt_ref[:, c * Q_CHAIN:(c + 1) * Q_CHAIN], preferred_element_type=F32)
        if key_minus_query is not None:
            krow = lax.broadcasted_iota(jnp.int32, s.shape, 0) + key_minus_query
            qcol = lax.broadcasted_iota(jnp.int32, s.shape, 1)
            s = jnp.where(krow <= qcol, s, -jnp.inf)
        return s

    def softmax_pv(j, c, s):
        cs = slice(c * Q_CHAIN, (c + 1) * Q_CHAIN)
        r = s
        while r.shape[0] > SUBLANES:
            half = r.shape[0] // 2
            r = jnp.maximum(r[:half], r[half:])
        m_old = m_s[:, cs]
        m_new = jnp.maximum(m_old, jnp.max(r, axis=0, keepdims=True))
        alpha = jnp.exp2(m_old - m_new)
        p = jnp.exp2(s - m_new).astype(BF16)
        m_s[:, cs] = m_new
        acc_s[:, cs] = alpha * acc_s[:, cs] + jnp.dot(vt_ref[j], p, preferred_element_type=F32)

    def run(todo):
        pending = {i: scores(*todo[i]) for i in range(min(QK_AHEAD, len(todo)))}
        for i, (j, c, _) in enumerate(todo):
            if i + QK_AHEAD < len(todo):
                pending[i + QK_AHEAD] = scores(*todo[i + QK_AHEAD])
            softmax_pv(j, c, pending.pop(i))

    def full_blocks(jj, carry):
        run([(jj * per_tile + u, c, None) for u in range(per_tile) for c in range(nchain)])
        return carry

    lax.fori_loop(0, nfull // per_tile, full_blocks, 0)

    diagonal = []
    for u in range(per_tile):
        for c in range(nchain):
            k0, q0 = u * tk, c * Q_CHAIN
            if k0 > q0 + Q_CHAIN - 1:
                continue
            diagonal.append((nfull + u, c, None if k0 + tk - 1 <= q0 else k0 - q0))
    run(diagonal)

    out_t = acc_s[:dv, :] / acc_s[dv:dv + 1, :]
    o_ref[...] = out_t.T.astype(o_ref.dtype)


def _attention(qt, k, vt):
    b, nh, dqk, s = qt.shape
    nkb, vrows, tk = vt.shape[2:]
    tq = T_Q
    return pl.pallas_call(
        _attn_body,
        grid=(b, nh, s // tq),
        in_specs=[pl.BlockSpec((None, None, dqk, tq), lambda bi, h, i: (bi, h, 0, i)),
                  pl.BlockSpec((None, None, s, dqk), lambda bi, h, i: (bi, h, 0, 0)),
                  pl.BlockSpec((None, None, nkb, vrows, tk), lambda bi, h, i: (bi, h, 0, 0, 0))],
        out_specs=pl.BlockSpec((None, tq, V_DIM), lambda bi, h, i: (bi, i, h)),
        out_shape=jax.ShapeDtypeStruct((b, s, nh * V_DIM), BF16),
        scratch_shapes=[pltpu.VMEM((1, tq), F32), pltpu.VMEM((vrows, tq), F32)],
        compiler_params=_params("parallel", "parallel", "arbitrary"),
        name="attention",
    )(qt, k, vt)


def _merge_body(o_ref, ga_ref, gb_ref, x_ref, wmo_ref, wout_ref, g2_ref, wrh_ref, wrl_ref, rb_ref,
                hrow_ref, rec_ref, cnt_ref, cnt_s, *, n_groups, per_group):
    @pl.when(pl.program_id(0) == 0)
    def _():
        cnt_s[...] = jnp.zeros_like(cnt_s)

    br_b = jnp.dot(o_ref[...], wmo_ref[...], preferred_element_type=F32)
    merged = ga_ref[...].astype(F32) + _sigmoid(gb_ref[...].astype(F32)) * br_b
    h = x_ref[...] + jnp.dot(merged.astype(BF16), wout_ref[...], preferred_element_type=F32)
    _natural_to_rows(hrow_ref, h)

    u = h * lax.rsqrt(jnp.mean(h * h, axis=-1, keepdims=True) + EPS) * g2_ref[...]
    u_hi = u.astype(BF16)
    u_lo = (u - u_hi.astype(F32)).astype(BF16)
    logits = (jnp.dot(u_hi, wrh_ref[...], preferred_element_type=F32)
              + jnp.dot(u_lo, wrh_ref[...], preferred_element_type=F32)
              + jnp.dot(u_hi, wrl_ref[...], preferred_element_type=F32)) + rb_ref[...]

    t, nl = logits.shape
    lane = lax.broadcasted_iota(jnp.int32, (t, nl), 1)
    neg = -jnp.inf

    def first_argmax(z, zmax):
        return jnp.min(jnp.where(z == zmax, lane, nl), axis=-1, keepdims=True)

    zg = jnp.where(lane < n_groups, logits, neg)
    mg = jnp.max(zg, axis=-1, keepdims=True)
    g_sel = first_argmax(zg, mg)
    g_prob = 1.0 / jnp.sum(jnp.exp(zg - mg), axis=-1, keepdims=True)
    lo = n_groups + g_sel * per_group
    ze = jnp.where((lane >= lo) & (lane < lo + per_group), logits, neg)
    m1 = jnp.max(ze, axis=-1, keepdims=True)
    i1 = first_argmax(ze, m1)
    ze2 = jnp.where(lane == i1, neg, ze)
    m2 = jnp.max(ze2, axis=-1, keepdims=True)
    i2 = first_argmax(ze2, m2)
    e21 = jnp.exp(m2 - m1)
    w1 = g_prob / (1.0 + e21)
    w2 = w1 * e21

    e1, e2 = i1 - n_groups, i2 - n_groups
    hot = jnp.where((lane == e1) | (lane == e2), 1.0, 0.0)
    cnt_s[...] = cnt_s[...] + jnp.sum(hot, axis=0, keepdims=True)
    cnt_ref[...] = cnt_s[...]

    rec = jnp.where(lane == 0, e1.astype(F32), 0.0)
    for col, val in ((1, e2.astype(F32)), (2, w1), (3, w2)):
        rec = jnp.where(lane == col, val, rec)
    rec_ref[...] = rec


def _merge(o2, ga2, gb2, x2, wmo, wout, g2, wrh, wrl, rb, n_groups, per_group):
    n, d = x2.shape
    t = T_MERGE
    consts = (wmo, wout, g2, wrh, wrl, rb)
    tile = pl.BlockSpec((t, d), lambda i: (i, 0))
    return pl.pallas_call(
        functools.partial(_merge_body, n_groups=n_groups, per_group=per_group),
        grid=(n // t,),
        in_specs=[tile, tile, tile, tile] + [_const_spec(a.shape) for a in consts],
        out_specs=[pl.BlockSpec((t * ROW_CHUNKS, LANES), lambda i: (i, 0)),
                   pl.BlockSpec((t, ROUTER_LANES), lambda i: (i, 0)),
                   pl.BlockSpec((SUBLANES, ROUTER_LANES), lambda i: (0, 0))],
        out_shape=[jax.ShapeDtypeStruct((n * ROW_CHUNKS, LANES), F32),
                   jax.ShapeDtypeStruct((n, ROUTER_LANES), F32),
                   jax.ShapeDtypeStruct((SUBLANES, ROUTER_LANES), F32)],
        scratch_shapes=[pltpu.VMEM((SUBLANES, ROUTER_LANES), F32)],
        compiler_params=_params("arbitrary"),
        name="merge_router",
    )(o2, ga2, gb2, x2, *consts)


def _slab(ref, row):
    return ref.at[pl.ds(pl.multiple_of(row * ROW_CHUNKS, ROW_CHUNKS), ROW_CHUNKS), :]


def _experts_body(tok_ref, out_ref, be_ref, nb_ref, h_hbm, g2_ref, w1_ref, w3_ref, w2_ref, y_hbm,
                  xbuf0, xbuf1, ybuf0, ybuf1, gsem, ssem):
    blk = pl.program_id(0)
    rows = MOE_BLOCK
    n_used = nb_ref[0]
    xbufs = (xbuf0, xbuf1)
    ybufs = (ybuf0, ybuf1)

    def start_gather(b, s):
        for r in range(rows):
            pltpu.make_async_copy(_slab(h_hbm, tok_ref[b * rows + r]), _slab(xbufs[s], r), gsem.at[s]).start()

    def wait_gather(s):
        pltpu.make_async_copy(h_hbm.at[pl.ds(0, rows * ROW_CHUNKS), :], xbufs[s], gsem.at[s]).wait()

    def start_scatter(b, s):
        for r in range(rows):
            pltpu.make_async_copy(_slab(ybufs[s], r), _slab(y_hbm, out_ref[b * rows + r]), ssem.at[s]).start()

    def wait_scatter(s):
        pltpu.make_async_copy(ybufs[s], y_hbm.at[pl.ds(0, rows * ROW_CHUNKS), :], ssem.at[s]).wait()

    @pl.when(blk == 0)
    def _():
        ybuf1[...] = jnp.zeros_like(ybuf1)
        spare = y_hbm.at[pl.ds(y_hbm.shape[0] - rows * ROW_CHUNKS, rows * ROW_CHUNKS), :]
        fill = pltpu.make_async_copy(ybuf1, spare, ssem.at[1])
        fill.start()
        fill.wait()

    @pl.when((blk == 0) & (n_used > 0))
    def _():
        start_gather(0, 0)

    def step(s, has_next):
        wait_gather(s)
        if has_next:
            start_gather(blk + 1, 1 - s)
        x = _rows_to_natural(xbufs[s], 0, rows)
        u = (x * lax.rsqrt(jnp.mean(x * x, axis=-1, keepdims=True) + EPS) * g2_ref[...]).astype(BF16)
        a = jnp.dot(u, w1_ref[...].astype(BF16), preferred_element_type=F32)
        c = jnp.dot(u, w3_ref[...].astype(BF16), preferred_element_type=F32)
        hmid = (a * _sigmoid(a) * c).astype(BF16)
        _natural_to_rows(ybufs[s], jnp.dot(hmid, w2_ref[...].astype(BF16), preferred_element_type=F32))

        @pl.when(blk > 0)
        def _():
            wait_scatter(1 - s)
        start_scatter(blk, s)

    for s in (0, 1):
        @pl.when((blk % 2 == s) & (blk + 1 < n_used))
        def _():
            step(s, True)

        @pl.when((blk % 2 == s) & (blk + 1 == n_used))
        def _():
            step(s, False)
            wait_scatter(s)


def _experts(tok_buf, out_rows, block_e, n_used, h_rows, g2, w1, w3, w2, n_out_rows):
    d = g2.shape[-1]
    rows = MOE_BLOCK
    nblk = tok_buf.shape[0] // rows
    de = w1.shape[-1]
    slab_buf = pltpu.VMEM((rows * ROW_CHUNKS, LANES), F32)
    grid_spec = pltpu.PrefetchScalarGridSpec(
        num_scalar_prefetch=4,
        grid=(nblk,),
        in_specs=[
            pl.BlockSpec(memory_space=pl.ANY),
            pl.BlockSpec(g2.shape, lambda i, *_: (0, 0)),
            pl.BlockSpec((None, d, de), lambda i, tok, out, be, nb: (be[i], 0, 0)),
            pl.BlockSpec((None, d, de), lambda i, tok, out, be, nb: (be[i], 0, 0)),
            pl.BlockSpec((None, de, d), lambda i, tok, out, be, nb: (be[i], 0, 0)),
        ],
        out_specs=pl.BlockSpec(memory_space=pl.ANY),
        scratch_shapes=[slab_buf, slab_buf, slab_buf, slab_buf,
                        pltpu.SemaphoreType.DMA((2,)), pltpu.SemaphoreType.DMA((2,))],
    )
    return pl.pallas_call(
        _experts_body,
        grid_spec=grid_spec,
        out_shape=jax.ShapeDtypeStruct((n_out_rows * ROW_CHUNKS, LANES), F32),
        compiler_params=_params("arbitrary"),
        name="experts",
    )(tok_buf, out_rows, block_e, n_used, h_rows, g2, w1, w3, w2)


def _combine_body(hrow_ref, rec_ref, y_ref, out_ref):
    t = rec_ref.shape[0]
    acc = _rows_to_natural(hrow_ref, 0, t)
    for kk in range(TOP_K):
        gate = rec_ref[:, TOP_K + kk:TOP_K + kk + 1]
        y = jnp.concatenate([y_ref[pl.ds(kk * ROW_CHUNKS + c, t, stride=TOP_K * ROW_CHUNKS), :]
                             for c in range(ROW_CHUNKS)], axis=1)
        acc = acc + gate * y
    out_ref[...] = acc


def _combine(h_rows, rec, y_rows):
    n = rec.shape[0]
    d = ROW_CHUNKS * LANES
    t = T_COMBINE
    return pl.pallas_call(
        _combine_body,
        grid=(n // t,),
        in_specs=[pl.BlockSpec((t * ROW_CHUNKS, LANES), lambda i: (i, 0)),
                  pl.BlockSpec((t, ROUTER_LANES), lambda i: (i, 0)),
                  pl.BlockSpec((TOP_K * t * ROW_CHUNKS, LANES), lambda i: (i, 0))],
        out_specs=pl.BlockSpec((t, d), lambda i: (i, 0)),
        out_shape=jax.ShapeDtypeStruct((n, d), F32),
        compiler_params=_params("parallel"),
        name="combine",
    )(h_rows, rec, y_rows)


def _dispatch_tables(expert_id, counts):
    n_experts = counts.shape[0]
    n = expert_id.shape[0]
    m = n * TOP_K
    m_pad = m + n_experts * MOE_BLOCK
    n_blocks = m_pad // MOE_BLOCK
    flat_e = expert_id.reshape(m)
    padded = ((counts + MOE_BLOCK - 1) // MOE_BLOCK) * MOE_BLOCK
    start = jnp.cumsum(counts) - counts
    pad_end = jnp.cumsum(padded)
    pad_start = pad_end - padded
    block_first = jnp.arange(n_blocks, dtype=jnp.int32) * MOE_BLOCK
    block_e = jnp.minimum(jnp.sum(pad_end[None, :] <= block_first[:, None], axis=1), n_experts - 1).astype(jnp.int32)
    n_used = (pad_end[-1] // MOE_BLOCK).astype(jnp.int32).reshape(1)
    order = jnp.argsort(flat_e, stable=True).astype(jnp.int32)
    lane = jnp.arange(MOE_BLOCK, dtype=jnp.int32)[None, :]
    off = (block_first - pad_start[block_e])[:, None] + lane
    valid = off < counts[block_e][:, None]
    assign = order[jnp.clip(start[block_e][:, None] + off, 0, m - 1)]
    tok_buf = jnp.where(valid, assign // TOP_K, 0).reshape(m_pad).astype(jnp.int32)
    out_rows = jnp.where(valid, assign, m + lane).reshape(m_pad).astype(jnp.int32)
    return tok_buf, out_rows, block_e, n_used


def _layer(x, positions, norm1_g, w_in, conv_w, conv_b, lru_wa, lru_ba, lru_wx, lru_bx, lru_lambda, w_rnn_o,
           q_norm_g, w_uq, kv_norm_g, w_ukv, qk_norm_q_g, qk_norm_k_g, w_mla_o, w_out, norm2_g,
           router_wg, router_bg, router_we, router_be, exp_w1, exp_w3, exp_w2):
    b, s, d = x.shape
    n = b * s
    d_rnn = conv_w.shape[-1]
    q_lora = q_norm_g.shape[-1]
    kv_lora = kv_norm_g.shape[-1]
    nh, half = N_HEADS, QK_ROPE // 2
    row = lambda a: a.reshape(1, -1).astype(F32)
    col = lambda a: a.reshape(-1, 1).astype(F32)

    o_xr, o_yr, o_cq, o_ckv, o_kpe, o_ga, o_gb = np.cumsum([0, d_rnn, d_rnn, q_lora, kv_lora, QK_ROPE, d])
    w_kpe = w_in[:, o_kpe:o_kpe + QK_ROPE]
    w_kpe_rot = jnp.concatenate([-w_kpe[:, half:], w_kpe[:, :half]], axis=1)
    w_in_k = jnp.concatenate([w_in[:, :o_kpe], w_in[:, o_ga:], w_kpe, w_kpe_rot], axis=1).astype(BF16)
    widths = (d_rnn, d_rnn, q_lora, kv_lora, d, d, 2 * QK_ROPE)
    dtypes = (BF16, BF16, F32, F32, BF16, BF16, F32)

    x2 = x.reshape(n, d)
    xr, yr, cq, ckv, ga, gb, kpe2 = _in_proj(x2, row(norm1_g), w_in_k, widths, dtypes)

    w_gate = jnp.concatenate([lru_wa, lru_wx], axis=-1).astype(BF16)
    gated_a = _rglru(xr.reshape(b, s, d_rnn), yr.reshape(b, s, d_rnn), ga.reshape(b, s, d),
                     conv_w.astype(F32), row(conv_b), w_gate, row(lru_ba), row(lru_bx), row(lru_lambda),
                     w_rnn_o.astype(BF16))

    def rot(v):
        return jnp.concatenate([-v[..., half:], v[..., :half]], axis=-1)

    def swap(v):
        return jnp.concatenate([v[..., half:], v[..., :half]], axis=-1)

    wq = w_uq.reshape(q_lora, nh, QK_DIM).transpose(1, 0, 2)
    wq_t = wq.transpose(0, 2, 1).astype(BF16)
    wqr_t = rot(wq[..., QK_NOPE:]).transpose(0, 2, 1).astype(BF16)
    wkv = w_ukv.reshape(kv_lora, nh, QK_NOPE + V_DIM).transpose(1, 0, 2)
    wk = wkv[..., :QK_NOPE].astype(BF16)
    wv_t = wkv[..., QK_NOPE:].transpose(0, 2, 1).astype(BF16)
    gq = qk_norm_q_g.astype(F32)
    gk = qk_norm_k_g.astype(F32)
    inv_freq = ROPE_THETA ** (-jnp.arange(0, QK_ROPE, 2, dtype=F32) / QK_ROPE)
    freq = jnp.concatenate([inv_freq, inv_freq])
    consts = (row(q_norm_g), row(kv_norm_g), wq_t, wqr_t, wk, wv_t,
              col(gq[:QK_NOPE]), col(gq[QK_NOPE:]), col(swap(gq[QK_NOPE:])),
              row(gk[:QK_NOPE]), row(gk[QK_NOPE:]), row(swap(gk[QK_NOPE:])), row(freq), col(freq))
    pos = positions.astype(F32)
    qt, k, vt = _mla_proj(cq.reshape(b, s, q_lora), ckv.reshape(b, s, kv_lora), kpe2.reshape(b, s, 2 * QK_ROPE),
                          pos.reshape(b, s, 1), pos.reshape(b, 1, s), consts)
    o = _attention(qt, k, vt)

    n_groups = router_wg.shape[-1]
    n_experts = router_we.shape[-1]
    per_group = n_experts // n_groups
    wr = jnp.concatenate([router_wg, router_we,
                          jnp.zeros((d, ROUTER_LANES - n_groups - n_experts), F32)], axis=1).astype(F32)
    wr_hi = wr.astype(BF16)
    wr_lo = (wr - wr_hi.astype(F32)).astype(BF16)
    rb = jnp.concatenate([router_bg, router_be,
                          jnp.zeros((ROUTER_LANES - n_groups - n_experts,), F32)]).reshape(1, ROUTER_LANES).astype(F32)
    h_rows, rec, cnt = _merge(o.reshape(n, d), gated_a.reshape(n, d), gb, x2, w_mla_o.astype(BF16),
                              w_out.astype(BF16), row(norm2_g), wr_hi, wr_lo, rb, n_groups, per_group)

    expert_id = rec[:, :TOP_K].astype(jnp.int32)
    counts = cnt[0, :n_experts].astype(jnp.int32)
    tok_buf, out_rows, block_e, n_used = _dispatch_tables(expert_id, counts)
    y_rows = _experts(tok_buf, out_rows, block_e, n_used, h_rows, row(norm2_g),
                      exp_w1, exp_w3, exp_w2, n * TOP_K + MOE_BLOCK)
    out = _combine(h_rows, rec, y_rows)
    return out.reshape(b, s, d)


def kernel(x, positions, norm1_g, w_in, conv_w, conv_b, lru_wa, lru_ba, lru_wx, lru_bx, lru_lambda, w_rnn_o, q_norm_g, w_uq, kv_norm_g, w_ukv, qk_norm_q_g, qk_norm_k_g, w_mla_o, w_out, norm2_g, router_wg, router_bg, router_we, router_be, exp_w1, exp_w3, exp_w2):
    args = (norm1_g, w_in, conv_w, conv_b, lru_wa, lru_ba, lru_wx, lru_bx, lru_lambda, w_rnn_o, q_norm_g, w_uq,
            kv_norm_g, w_ukv, qk_norm_q_g, qk_norm_k_g, w_mla_o, w_out, norm2_g, router_wg, router_bg, router_we,
            router_be, exp_w1, exp_w3, exp_w2)
    depth = norm1_g.shape[0]
    h = x
    for l in range(depth):
        h = _layer(h, positions, *[a[l] for a in args])
    return h
```

```python
import functools
import math

import numpy as np
import jax
import jax.numpy as jnp
from jax import lax
from jax.experimental import pallas as pl
from jax.experimental.pallas import tpu as pltpu

F32 = jnp.float32
BF16 = jnp.bfloat16
EPS = 1e-6

N_HEADS = 8
QK_NOPE = 128
QK_ROPE = 64
QK_DIM = QK_NOPE + QK_ROPE
V_DIM = 128
ROPE_THETA = 10000.0
LRU_C = 8.0
TOP_K = 2
TOP_K_SHIFT = 1
MOE_BLOCK = 128
ROUTER_LANES = 128

SUBLANES = 8
LANES = 128
VMEM_LIMIT_BYTES = 56 * 1024 * 1024

T_INPROJ = 256
T_RGLRU = 256
T_KV = 512
T_MLAPROJ = T_KV
T_Q = 2048
Q_CHAIN = 512
QK_AHEAD = 3
V_ROWS = V_DIM + 16
T_MERGE = 256
T_COMBINE = 256
ROW_CHUNKS = 8


def _params(*sem):
    return pltpu.CompilerParams(dimension_semantics=sem, vmem_limit_bytes=VMEM_LIMIT_BYTES)


def _sigmoid(z):
    return 0.5 * jnp.tanh(0.5 * z) + 0.5


def _const_spec(shape):
    n = len(shape)
    return pl.BlockSpec(shape, lambda *_: (0,) * n)


def _rows_to_natural(ref, first, rows):
    return jnp.concatenate([ref[pl.ds(first + c, rows, stride=ROW_CHUNKS), :] for c in range(ROW_CHUNKS)], axis=1)


def _natural_to_rows(ref, value):
    rows = value.shape[0]
    for c in range(ROW_CHUNKS):
        ref[pl.ds(c, rows, stride=ROW_CHUNKS), :] = value[:, c * LANES:(c + 1) * LANES]


def _inproj_body(x_ref, g_ref, w_ref, *out_refs):
    x = x_ref[...]
    u = x * lax.rsqrt(jnp.mean(x * x, axis=-1, keepdims=True) + EPS) * g_ref[...]
    u = u.astype(BF16)
    off = 0
    for ref in out_refs:
        n = ref.shape[-1]
        ref[...] = jnp.dot(u, w_ref[:, off:off + n], preferred_element_type=F32).astype(ref.dtype)
        off += n


def _in_proj(x2, g, w, widths, dtypes):
    n, d = x2.shape
    t = T_INPROJ
    return pl.pallas_call(
        _inproj_body,
        grid=(n // t,),
        in_specs=[pl.BlockSpec((t, d), lambda i: (i, 0)), _const_spec(g.shape), _const_spec(w.shape)],
        out_specs=[pl.BlockSpec((t, c), lambda i: (i, 0)) for c in widths],
        out_shape=[jax.ShapeDtypeStruct((n, c), dt) for c, dt in zip(widths, dtypes)],
        compiler_params=_params("parallel"),
        name="in_proj",
    )(x2, g, w)


def _rglru_body(xr_ref, yr_ref, ga_ref, cw_ref, cb_ref, wg_ref, ba_ref, bx_ref, lam_ref, wo_ref,
                out_ref, xbuf, a_s, b_s, h_s, hcar):
    t, c = xr_ref.shape
    nblk, bw, _ = wg_ref.shape

    @pl.when(pl.program_id(1) == 0)
    def _():
        xbuf[0:SUBLANES, :] = jnp.zeros((SUBLANES, c), F32)
        hcar[...] = jnp.zeros_like(hcar)

    xbuf[SUBLANES:, :] = xr_ref[...].astype(F32)
    lam = lam_ref[...]
    sp = jnp.maximum(-lam, 0.0) + jnp.log(1.0 + jnp.exp(-jnp.abs(lam)))
    rowmod = lax.broadcasted_iota(jnp.int32, (t, bw), 0) % SUBLANES

    for n in range(nblk):
        sl = slice(n * bw, (n + 1) * bw)
        xc = cb_ref[:, sl] + cw_ref[3:4, sl] * xbuf[SUBLANES:SUBLANES + t, sl]
        for s in (1, 2, 3):
            xc = xc + cw_ref[3 - s:4 - s, sl] * xbuf[SUBLANES - s:SUBLANES - s + t, sl]
        g = jnp.dot(xc.astype(BF16), wg_ref[n], preferred_element_type=F32)
        r = _sigmoid(g[:, :bw] + ba_ref[:, sl])
        i = _sigmoid(g[:, bw:] + bx_ref[:, sl])
        log_a = -LRU_C * r * sp[:, sl]
        a = jnp.exp(log_a)
        b = xc * i * jnp.sqrt(-jnp.tanh(log_a) * (a * a + 1.0))
        for d in (1, 2, 4):
            keep = rowmod >= d
            a_sh = pltpu.roll(a, d, 0)
            b_sh = pltpu.roll(b, d, 0)
            b = jnp.where(keep, a * b_sh + b, b)
            a = jnp.where(keep, a * a_sh, a)
        a_s[:, sl] = a
        b_s[:, sl] = b

    xbuf[0:SUBLANES, :] = xbuf[t:t + SUBLANES, :]

    def group(gi, hb):
        rows = pl.ds(pl.multiple_of(gi * SUBLANES, SUBLANES), SUBLANES)
        h8 = a_s[rows, :] * hb + b_s[rows, :]
        h_s[rows, :] = h8
        return jnp.broadcast_to(h8[SUBLANES - 1:SUBLANES, :], (SUBLANES, c))

    hcar[...] = lax.fori_loop(0, t // SUBLANES, group, hcar[...])

    y = yr_ref[...].astype(F32)
    gelu = 0.5 * y * (1.0 + jnp.tanh(math.sqrt(2.0 / math.pi) * (y + 0.044715 * (y * y * y))))
    hg = (h_s[...] * gelu).astype(BF16)
    br = jnp.dot(hg, wo_ref[...], preferred_element_type=F32)
    out_ref[...] = (_sigmoid(ga_ref[...].astype(F32)) * br).astype(out_ref.dtype)


def _rglru(xr, yr, ga, cw, cb, wg, ba, bx, lam, wo):
    b, s, c = xr.shape
    t = T_RGLRU
    tile = pl.BlockSpec((None, t, c), lambda bi, j: (bi, j, 0))
    consts = (cw, cb, wg, ba, bx, lam, wo)
    return pl.pallas_call(
        _rglru_body,
        grid=(b, s // t),
        in_specs=[tile, tile, tile] + [_const_spec(a.shape) for a in consts],
        out_specs=tile,
        out_shape=jax.ShapeDtypeStruct((b, s, c), BF16),
        scratch_shapes=[
            pltpu.VMEM((t + SUBLANES, c), F32),
            pltpu.VMEM((t, c), F32),
            pltpu.VMEM((t, c), F32),
            pltpu.VMEM((t, c), F32),
            pltpu.VMEM((SUBLANES, c), F32),
        ],
        compiler_params=_params("parallel", "arbitrary"),
        name="rglru",
    )(xr, yr, ga, *consts)


def _mlaproj_body(cq_ref, ckv_ref, kpe_ref, posc_ref, posr_ref, qg_ref, kvg_ref, wqt_ref, wqrt_ref, wk_ref, wvt_ref,
                  gqn_ref, gqr_ref, gqrr_ref, gkn_ref, gkr_ref, gkrr_ref, frow_ref, fcol_ref,
                  qt_ref, k_ref, vt_ref):
    nh = wqt_ref.shape[0]
    t = cq_ref.shape[0]

    def latent_norm(ref, g_ref):
        z = ref[...]
        return z * lax.rsqrt(jnp.mean(z * z, axis=-1, keepdims=True) + EPS) * g_ref[...]

    cqn_t = latent_norm(cq_ref, qg_ref).T.astype(BF16)
    ckvn = latent_norm(ckv_ref, kvg_ref)
    ckvn_t = ckvn.T.astype(BF16)
    ckvn = ckvn.astype(BF16)

    ang_t = fcol_ref[...] * posr_ref[...]
    gr_cos = gqr_ref[...] * jnp.cos(ang_t)
    grr_sin = gqrr_ref[...] * jnp.sin(ang_t)
    qscale = QK_DIM ** -0.5 * math.log2(math.e)
    for h in range(nh):
        qh = jnp.dot(wqt_ref[h], cqn_t, preferred_element_type=F32)
        qr = jnp.dot(wqrt_ref[h], cqn_t, preferred_element_type=F32)
        rstd = lax.rsqrt(jnp.mean(qh * qh, axis=0, keepdims=True) + EPS) * qscale
        qt_ref[h, :QK_NOPE, :] = (qh[:QK_NOPE] * gqn_ref[...] * rstd).astype(qt_ref.dtype)
        qt_ref[h, QK_NOPE:, :] = ((qh[QK_NOPE:] * gr_cos + qr * grr_sin) * rstd).astype(qt_ref.dtype)

    ang = posc_ref[...] * frow_ref[...]
    kpe = kpe_ref[:, :QK_ROPE]
    kpr = kpe_ref[:, QK_ROPE:]
    k_rot = kpe * gkr_ref[...] * jnp.cos(ang) + kpr * gkrr_ref[...] * jnp.sin(ang)
    ss_pe = jnp.sum(kpe * kpe, axis=-1, keepdims=True)
    ones_rows = (lax.broadcasted_iota(jnp.int32, (V_ROWS - V_DIM, t), 0) == 0).astype(vt_ref.dtype)
    for h in range(nh):
        kn = jnp.dot(ckvn, wk_ref[h], preferred_element_type=F32)
        rstd = lax.rsqrt((jnp.sum(kn * kn, axis=-1, keepdims=True) + ss_pe) * (1.0 / QK_DIM) + EPS)
        k_ref[h, :, :QK_NOPE] = (kn * gkn_ref[...] * rstd).astype(k_ref.dtype)
        k_ref[h, :, QK_NOPE:] = (k_rot * rstd).astype(k_ref.dtype)
        vt_ref[h, :V_DIM, :] = jnp.dot(wvt_ref[h], ckvn_t, preferred_element_type=F32).astype(vt_ref.dtype)
        vt_ref[h, V_DIM:, :] = ones_rows


def _mla_proj(cq, ckv, kpe2, pos_col, pos_row, consts):
    b, s, _ = cq.shape
    t = T_MLAPROJ
    nh = N_HEADS

    def tile(c):
        return pl.BlockSpec((None, t, c), lambda bi, j: (bi, j, 0))

    return pl.pallas_call(
        _mlaproj_body,
        grid=(b, s // t),
        in_specs=[tile(cq.shape[-1]), tile(ckv.shape[-1]), tile(kpe2.shape[-1]), tile(1),
                  pl.BlockSpec((None, 1, t), lambda bi, j: (bi, 0, j))]
        + [_const_spec(a.shape) for a in consts],
        out_specs=[pl.BlockSpec((None, nh, QK_DIM, t), lambda bi, j: (bi, 0, 0, j)),
                   pl.BlockSpec((None, nh, t, QK_DIM), lambda bi, j: (bi, 0, j, 0)),
                   pl.BlockSpec((None, nh, None, V_ROWS, t), lambda bi, j: (bi, 0, j, 0, 0))],
        out_shape=[jax.ShapeDtypeStruct((b, nh, QK_DIM, s), BF16),
                   jax.ShapeDtypeStruct((b, nh, s, QK_DIM), BF16),
                   jax.ShapeDtypeStruct((b, nh, s // t, V_ROWS, t), BF16)],
        compiler_params=_params("parallel", "parallel"),
        name="mla_proj",
    )(cq, ckv, kpe2, pos_col, pos_row, *consts)


def _attn_body(qt_ref, k_ref, vt_ref, o_ref, m_s, acc_s):
    tq = qt_ref.shape[1]
    tk = vt_ref.shape[-1]
    dv = o_ref.shape[-1]
    nchain = tq // Q_CHAIN
    per_tile = tq // tk
    nfull = pl.program_id(2) * per_tile
    m_s[...] = jnp.full(m_s.shape, -jnp.inf, F32)
    acc_s[...] = jnp.zeros(acc_s.shape, F32)

    def scores(j, c, key_minus_query):
        kb = k_ref[pl.ds(pl.multiple_of(j * tk, tk), tk), :]
        s = jnp.dot(kb, qt_ref[:, c * Q_CHAIN:(c + 1) * Q_CHAIN], preferred_element_type=F32)
        if key_minus_query is not None:
            krow = lax.broadcasted_iota(jnp.int32, s.shape, 0) + key_minus_query
            qcol = lax.broadcasted_iota(jnp.int32, s.shape, 1)
            s = jnp.where(krow <= qcol, s, -jnp.inf)
        return s

    def softmax_pv(j, c, s):
        cs = slice(c * Q_CHAIN, (c + 1) * Q_CHAIN)
        r = s
        while r.shape[0] > SUBLANES:
            half = r.shape[0] // 2
            r = jnp.maximum(r[:half], r[half:])
        m_old = m_s[:, cs]
        m_new = jnp.maximum(m_old, jnp.max(r, axis=0, keepdims=True))
        alpha = jnp.exp2(m_old - m_new)
        p = jnp.exp2(s - m_new).astype(BF16)
        m_s[:, cs] = m_new
        acc_s[:, cs] = alpha * acc_s[:, cs] + jnp.dot(vt_ref[j], p, preferred_element_type=F32)

    def run(todo):
        pending = {i: scores(*todo[i]) for i in range(min(QK_AHEAD, len(todo)))}
        for i, (j, c, _) in enumerate(todo):
            if i + QK_AHEAD < len(todo):
                pending[i + QK_AHEAD] = scores(*todo[i + QK_AHEAD])
            softmax_pv(j, c, pending.pop(i))

    def full_blocks(jj, carry):
        run([(jj * per_tile + u, c, None) for u in range(per_tile) for c in range(nchain)])
        return carry

    lax.fori_loop(0, nfull // per_tile, full_blocks, 0)

    diagonal = []
    for u in range(per_tile):
        for c in range(nchain):
            k0, q0 = u * tk, c * Q_CHAIN
            if k0 > q0 + Q_CHAIN - 1:
                continue
            diagonal.append((nfull + u, c, None if k0 + tk - 1 <= q0 else k0 - q0))
    run(diagonal)

    out_t = acc_s[:dv, :] / acc_s[dv:dv + 1, :]
    o_ref[...] = out_t.T.astype(o_ref.dtype)


def _attention(qt, k, vt):
    b, nh, dqk, s = qt.shape
    nkb, vrows, tk = vt.shape[2:]
    tq = T_Q
    return pl.pallas_call(
        _attn_body,
        grid=(b, nh, s // tq),
        in_specs=[pl.BlockSpec((None, None, dqk, tq), lambda bi, h, i: (bi, h, 0, i)),
                  pl.BlockSpec((None, None, s, dqk), lambda bi, h, i: (bi, h, 0, 0)),
                  pl.BlockSpec((None, None, nkb, vrows, tk), lambda bi, h, i: (bi, h, 0, 0, 0))],
        out_specs=pl.BlockSpec((None, tq, V_DIM), lambda bi, h, i: (bi, i, h)),
        out_shape=jax.ShapeDtypeStruct((b, s, nh * V_DIM), BF16),
        scratch_shapes=[pltpu.VMEM((1, tq), F32), pltpu.VMEM((vrows, tq), F32)],
        compiler_params=_params("parallel", "parallel", "arbitrary"),
        name="attention",
    )(qt, k, vt)


def _merge_body(o_ref, ga_ref, gb_ref, x_ref, wmo_ref, wout_ref, g2_ref, wrh_ref, wrl_ref, rb_ref,
                hrow_ref, rec_ref, cnt_ref, cnt_s, *, n_groups, per_group):
    @pl.when(pl.program_id(0) == 0)
    def _():
        cnt_s[...] = jnp.zeros_like(cnt_s)

    br_b = jnp.dot(o_ref[...], wmo_ref[...], preferred_element_type=F32)
    merged = ga_ref[...].astype(F32) + _sigmoid(gb_ref[...].astype(F32)) * br_b
    h = x_ref[...] + jnp.dot(merged.astype(BF16), wout_ref[...], preferred_element_type=F32)
    _natural_to_rows(hrow_ref, h)

    u = h * lax.rsqrt(jnp.mean(h * h, axis=-1, keepdims=True) + EPS) * g2_ref[...]
    u_hi = u.astype(BF16)
    u_lo = (u - u_hi.astype(F32)).astype(BF16)
    logits = (jnp.dot(u_hi, wrh_ref[...], preferred_element_type=F32)
              + jnp.dot(u_lo, wrh_ref[...], preferred_element_type=F32)
              + jnp.dot(u_hi, wrl_ref[...], preferred_element_type=F32)) + rb_ref[...]

    t, nl = logits.shape
    lane = lax.broadcasted_iota(jnp.int32, (t, nl), 1)
    neg = -jnp.inf

    def first_argmax(z, zmax):
        return jnp.min(jnp.where(z == zmax, lane, nl), axis=-1, keepdims=True)

    zg = jnp.where(lane < n_groups, logits, neg)
    mg = jnp.max(zg, axis=-1, keepdims=True)
    g_sel = first_argmax(zg, mg)
    g_prob = 1.0 / jnp.sum(jnp.exp(zg - mg), axis=-1, keepdims=True)
    lo = n_groups + g_sel * per_group
    ze = jnp.where((lane >= lo) & (lane < lo + per_group), logits, neg)
    m1 = jnp.max(ze, axis=-1, keepdims=True)
    i1 = first_argmax(ze, m1)
    ze2 = jnp.where(lane == i1, neg, ze)
    m2 = jnp.max(ze2, axis=-1, keepdims=True)
    i2 = first_argmax(ze2, m2)
    e21 = jnp.exp(m2 - m1)
    w1 = g_prob / (1.0 + e21)
    w2 = w1 * e21

    e1, e2 = i1 - n_groups, i2 - n_groups
    hot = jnp.where((lane == e1) | (lane == e2), 1.0, 0.0)
    cnt_s[...] = cnt_s[...] + jnp.sum(hot, axis=0, keepdims=True)
    cnt_ref[...] = cnt_s[...]

    rec = jnp.where(lane == 0, e1.astype(F32), 0.0)
    for col, val in ((1, e2.astype(F32)), (2, w1), (3, w2)):
        rec = jnp.where(lane == col, val, rec)
    rec_ref[...] = rec


def _merge(o2, ga2, gb2, x2, wmo, wout, g2, wrh, wrl, rb, n_groups, per_group):
    n, d = x2.shape
    t = T_MERGE
    consts = (wmo, wout, g2, wrh, wrl, rb)
    tile = pl.BlockSpec((t, d), lambda i: (i, 0))
    return pl.pallas_call(
        functools.partial(_merge_body, n_groups=n_groups, per_group=per_group),
        grid=(n // t,),
        in_specs=[tile, tile, tile, tile] + [_const_spec(a.shape) for a in consts],
        out_specs=[pl.BlockSpec((t * ROW_CHUNKS, LANES), lambda i: (i, 0)),
                   pl.BlockSpec((t, ROUTER_LANES), lambda i: (i, 0)),
                   pl.BlockSpec((SUBLANES, ROUTER_LANES), lambda i: (0, 0))],
        out_shape=[jax.ShapeDtypeStruct((n * ROW_CHUNKS, LANES), F32),
                   jax.ShapeDtypeStruct((n, ROUTER_LANES), F32),
                   jax.ShapeDtypeStruct((SUBLANES, ROUTER_LANES), F32)],
        scratch_shapes=[pltpu.VMEM((SUBLANES, ROUTER_LANES), F32)],
        compiler_params=_params("arbitrary"),
        name="merge_router",
    )(o2, ga2, gb2, x2, *consts)


def _slab(ref, row):
    return ref.at[pl.ds(pl.multiple_of(row * ROW_CHUNKS, ROW_CHUNKS), ROW_CHUNKS), :]


def _experts_body(order_ref, first_ref, nvalid_ref, be_ref, nb_ref, h_hbm, g2_ref, w1_ref, w3_ref, w2_ref, y_hbm,
                  xbuf0, xbuf1, ybuf0, ybuf1, gsem, ssem):
    blk = pl.program_id(0)
    rows = MOE_BLOCK
    n_used = nb_ref[0]
    n_assign = order_ref.shape[0]
    xbufs = (xbuf0, xbuf1)
    ybufs = (ybuf0, ybuf1)

    def assignment(b, r):
        return order_ref[jnp.minimum(first_ref[b] + r, n_assign - 1)], r < nvalid_ref[b]

    def start_gather(b, s):
        for r in range(rows):
            a, valid = assignment(b, r)
            tok = jnp.where(valid, lax.shift_right_logical(a, TOP_K_SHIFT), 0)
            pltpu.make_async_copy(_slab(h_hbm, tok), _slab(xbufs[s], r), gsem.at[s]).start(priority=r % 2)

    def wait_gather(s):
        pltpu.make_async_copy(h_hbm.at[pl.ds(0, rows * ROW_CHUNKS), :], xbufs[s], gsem.at[s]).wait()

    def start_scatter(b, s):
        for r in range(rows):
            a, valid = assignment(b, r)
            dst = jnp.where(valid, a, n_assign + r)
            pltpu.make_async_copy(_slab(ybufs[s], r), _slab(y_hbm, dst), ssem.at[s]).start(priority=r % 2)

    def wait_scatter(s):
        pltpu.make_async_copy(ybufs[s], y_hbm.at[pl.ds(0, rows * ROW_CHUNKS), :], ssem.at[s]).wait()

    @pl.when(blk == 0)
    def _():
        ybuf1[...] = jnp.zeros_like(ybuf1)
        spare = y_hbm.at[pl.ds(y_hbm.shape[0] - rows * ROW_CHUNKS, rows * ROW_CHUNKS), :]
        fill = pltpu.make_async_copy(ybuf1, spare, ssem.at[1])
        fill.start()
        fill.wait()

    @pl.when((blk == 0) & (n_used > 0))
    def _():
        start_gather(0, 0)

    def step(s, has_next):
        wait_gather(s)
        if has_next:
            start_gather(blk + 1, 1 - s)
        x = _rows_to_natural(xbufs[s], 0, rows)
        u = (x * lax.rsqrt(jnp.mean(x * x, axis=-1, keepdims=True) + EPS) * g2_ref[...]).astype(BF16)
        a = jnp.dot(u, w1_ref[...].astype(BF16), preferred_element_type=F32)
        c = jnp.dot(u, w3_ref[...].astype(BF16), preferred_element_type=F32)
        hmid = (a * _sigmoid(a) * c).astype(BF16)
        _natural_to_rows(ybufs[s], jnp.dot(hmid, w2_ref[...].astype(BF16), preferred_element_type=F32))

        @pl.when(blk > 0)
        def _():
            wait_scatter(1 - s)
        start_scatter(blk, s)

    for s in (0, 1):
        @pl.when((blk % 2 == s) & (blk + 1 < n_used))
        def _():
            step(s, True)

        @pl.when((blk % 2 == s) & (blk + 1 == n_used))
        def _():
            step(s, False)
            wait_scatter(s)


def _experts(order, block_first, block_nvalid, block_e, n_used, h_rows, g2, w1, w3, w2):
    d = g2.shape[-1]
    rows = MOE_BLOCK
    nblk = block_e.shape[0]
    de = w1.shape[-1]
    n_out_rows = order.shape[0] + rows
    slab_buf = pltpu.VMEM((rows * ROW_CHUNKS, LANES), F32)
    grid_spec = pltpu.PrefetchScalarGridSpec(
        num_scalar_prefetch=5,
        grid=(nblk,),
        in_specs=[
            pl.BlockSpec(memory_space=pl.ANY),
            pl.BlockSpec(g2.shape, lambda i, *_: (0, 0)),
            pl.BlockSpec((None, d, de), lambda i, order, first, nvalid, be, nb: (be[i], 0, 0)),
            pl.BlockSpec((None, d, de), lambda i, order, first, nvalid, be, nb: (be[i], 0, 0)),
            pl.BlockSpec((None, de, d), lambda i, order, first, nvalid, be, nb: (be[i], 0, 0)),
        ],
        out_specs=pl.BlockSpec(memory_space=pl.ANY),
        scratch_shapes=[slab_buf, slab_buf, slab_buf, slab_buf,
                        pltpu.SemaphoreType.DMA((2,)), pltpu.SemaphoreType.DMA((2,))],
    )
    return pl.pallas_call(
        _experts_body,
        grid_spec=grid_spec,
        out_shape=jax.ShapeDtypeStruct((n_out_rows * ROW_CHUNKS, LANES), F32),
        compiler_params=_params("arbitrary"),
        name="experts",
    )(order, block_first, block_nvalid, block_e, n_used, h_rows, g2, w1, w3, w2)


def _combine_body(hrow_ref, rec_ref, y_ref, out_ref):
    t = rec_ref.shape[0]
    acc = _rows_to_natural(hrow_ref, 0, t)
    for kk in range(TOP_K):
        gate = rec_ref[:, TOP_K + kk:TOP_K + kk + 1]
        y = jnp.concatenate([y_ref[pl.ds(kk * ROW_CHUNKS + c, t, stride=TOP_K * ROW_CHUNKS), :]
                             for c in range(ROW_CHUNKS)], axis=1)
        acc = acc + gate * y
    out_ref[...] = acc


def _combine(h_rows, rec, y_rows):
    n = rec.shape[0]
    d = ROW_CHUNKS * LANES
    t = T_COMBINE
    return pl.pallas_call(
        _combine_body,
        grid=(n // t,),
        in_specs=[pl.BlockSpec((t * ROW_CHUNKS, LANES), lambda i: (i, 0)),
                  pl.BlockSpec((t, ROUTER_LANES), lambda i: (i, 0)),
                  pl.BlockSpec((TOP_K * t * ROW_CHUNKS, LANES), lambda i: (i, 0))],
        out_specs=pl.BlockSpec((t, d), lambda i: (i, 0)),
        out_shape=jax.ShapeDtypeStruct((n, d), F32),
        compiler_params=_params("parallel"),
        name="combine",
    )(h_rows, rec, y_rows)


def _dispatch_tables(expert_id, counts):
    n_experts = counts.shape[0]
    n = expert_id.shape[0]
    m = n * TOP_K
    m_pad = m + n_experts * MOE_BLOCK
    n_blocks = m_pad // MOE_BLOCK
    flat_e = expert_id.reshape(m)
    padded = ((counts + MOE_BLOCK - 1) // MOE_BLOCK) * MOE_BLOCK
    start = jnp.cumsum(counts) - counts
    pad_end = jnp.cumsum(padded)
    pad_start = pad_end - padded
    block_row = jnp.arange(n_blocks, dtype=jnp.int32) * MOE_BLOCK
    block_e = jnp.minimum(jnp.sum(pad_end[None, :] <= block_row[:, None], axis=1), n_experts - 1).astype(jnp.int32)
    n_used = (pad_end[-1] // MOE_BLOCK).astype(jnp.int32).reshape(1)
    order = jnp.argsort(flat_e, stable=True).astype(jnp.int32)
    off = block_row - pad_start[block_e]
    block_first = (start[block_e] + off).astype(jnp.int32)
    block_nvalid = jnp.clip(counts[block_e] - off, 0, MOE_BLOCK).astype(jnp.int32)
    return order, block_first, block_nvalid, block_e, n_used


def _layer(x, positions, norm1_g, w_in, conv_w, conv_b, lru_wa, lru_ba, lru_wx, lru_bx, lru_lambda, w_rnn_o,
           q_norm_g, w_uq, kv_norm_g, w_ukv, qk_norm_q_g, qk_norm_k_g, w_mla_o, w_out, norm2_g,
           router_wg, router_bg, router_we, router_be, exp_w1, exp_w3, exp_w2):
    b, s, d = x.shape
    n = b * s
    d_rnn = conv_w.shape[-1]
    q_lora = q_norm_g.shape[-1]
    kv_lora = kv_norm_g.shape[-1]
    nh, half = N_HEADS, QK_ROPE // 2
    row = lambda a: a.reshape(1, -1).astype(F32)
    col = lambda a: a.reshape(-1, 1).astype(F32)

    o_xr, o_yr, o_cq, o_ckv, o_kpe, o_ga, o_gb = np.cumsum([0, d_rnn, d_rnn, q_lora, kv_lora, QK_ROPE, d])
    w_kpe = w_in[:, o_kpe:o_kpe + QK_ROPE]
    w_kpe_rot = jnp.concatenate([-w_kpe[:, half:], w_kpe[:, :half]], axis=1)
    w_in_k = jnp.concatenate([w_in[:, :o_kpe], w_in[:, o_ga:], w_kpe, w_kpe_rot], axis=1).astype(BF16)
    widths = (d_rnn, d_rnn, q_lora, kv_lora, d, d, 2 * QK_ROPE)
    dtypes = (BF16, BF16, F32, F32, BF16, BF16, F32)

    x2 = x.reshape(n, d)
    xr, yr, cq, ckv, ga, gb, kpe2 = _in_proj(x2, row(norm1_g), w_in_k, widths, dtypes)

    w_gate = jnp.concatenate([lru_wa, lru_wx], axis=-1).astype(BF16)
    gated_a = _rglru(xr.reshape(b, s, d_rnn), yr.reshape(b, s, d_rnn), ga.reshape(b, s, d),
                     conv_w.astype(F32), row(conv_b), w_gate, row(lru_ba), row(lru_bx), row(lru_lambda),
                     w_rnn_o.astype(BF16))

    def rot(v):
        return jnp.concatenate([-v[..., half:], v[..., :half]], axis=-1)

    def swap(v):
        return jnp.concatenate([v[..., half:], v[..., :half]], axis=-1)

    wq = w_uq.reshape(q_lora, nh, QK_DIM).transpose(1, 0, 2)
    wq_t = wq.transpose(0, 2, 1).astype(BF16)
    wqr_t = rot(wq[..., QK_NOPE:]).transpose(0, 2, 1).astype(BF16)
    wkv = w_ukv.reshape(kv_lora, nh, QK_NOPE + V_DIM).transpose(1, 0, 2)
    wk = wkv[..., :QK_NOPE].astype(BF16)
    wv_t = wkv[..., QK_NOPE:].transpose(0, 2, 1).astype(BF16)
    gq = qk_norm_q_g.astype(F32)
    gk = qk_norm_k_g.astype(F32)
    inv_freq = ROPE_THETA ** (-jnp.arange(0, QK_ROPE, 2, dtype=F32) / QK_ROPE)
    freq = jnp.concatenate([inv_freq, inv_freq])
    consts = (row(q_norm_g), row(kv_norm_g), wq_t, wqr_t, wk, wv_t,
              col(gq[:QK_NOPE]), col(gq[QK_NOPE:]), col(swap(gq[QK_NOPE:])),
              row(gk[:QK_NOPE]), row(gk[QK_NOPE:]), row(swap(gk[QK_NOPE:])), row(freq), col(freq))
    pos = positions.astype(F32)
    qt, k, vt = _mla_proj(cq.reshape(b, s, q_lora), ckv.reshape(b, s, kv_lora), kpe2.reshape(b, s, 2 * QK_ROPE),
                          pos.reshape(b, s, 1), pos.reshape(b, 1, s), consts)
    o = _attention(qt, k, vt)

    n_groups = router_wg.shape[-1]
    n_experts = router_we.shape[-1]
    per_group = n_experts // n_groups
    wr = jnp.concatenate([router_wg, router_we,
                          jnp.zeros((d, ROUTER_LANES - n_groups - n_experts), F32)], axis=1).astype(F32)
    wr_hi = wr.astype(BF16)
    wr_lo = (wr - wr_hi.astype(F32)).astype(BF16)
    rb = jnp.concatenate([router_bg, router_be,
                          jnp.zeros((ROUTER_LANES - n_groups - n_experts,), F32)]).reshape(1, ROUTER_LANES).astype(F32)
    h_rows, rec, cnt = _merge(o.reshape(n, d), gated_a.reshape(n, d), gb, x2, w_mla_o.astype(BF16),
                              w_out.astype(BF16), row(norm2_g), wr_hi, wr_lo, rb, n_groups, per_group)

    expert_id = rec[:, :TOP_K].astype(jnp.int32)
    counts = cnt[0, :n_experts].astype(jnp.int32)
    order, block_first, block_nvalid, block_e, n_used = _dispatch_tables(expert_id, counts)
    y_rows = _experts(order, block_first, block_nvalid, block_e, n_used, h_rows, row(norm2_g),
                      exp_w1, exp_w3, exp_w2)
    out = _combine(h_rows, rec, y_rows)
    return out.reshape(b, s, d)


def kernel(x, positions, norm1_g, w_in, conv_w, conv_b, lru_wa, lru_ba, lru_wx, lru_bx, lru_lambda, w_rnn_o, q_norm_g, w_uq, kv_norm_g, w_ukv, qk_norm_q_g, qk_norm_k_g, w_mla_o, w_out, norm2_g, router_wg, router_bg, router_we, router_be, exp_w1, exp_w3, exp_w2):
    args = (norm1_g, w_in, conv_w, conv_b, lru_wa, lru_ba, lru_wx, lru_bx, lru_lambda, w_rnn_o, q_norm_g, w_uq,
            kv_norm_g, w_ukv, qk_norm_q_g, qk_norm_k_g, w_mla_o, w_out, norm2_g, router_wg, router_bg, router_we,
            router_be, exp_w1, exp_w3, exp_w2)
    depth = norm1_g.shape[0]
    h = x
    for l in range(depth):
        h = _layer(h, positions, *[a[l] for a in args])
    return h
```

```python
import functools
import math

import numpy as np
import jax
import jax.numpy as jnp
from jax import lax
from jax.experimental import pallas as pl
from jax.experimental.pallas import tpu as pltpu

F32 = jnp.float32
BF16 = jnp.bfloat16
EPS = 1e-6

N_HEADS = 8
QK_NOPE = 128
QK_ROPE = 64
QK_DIM = QK_NOPE + QK_ROPE
V_DIM = 128
ROPE_THETA = 10000.0
LRU_C = 8.0
TOP_K = 2
TOP_K_SHIFT = 1
MOE_BLOCK = 128
ROUTER_LANES = 128

SUBLANES = 8
LANES = 128
VMEM_LIMIT_BYTES = 56 * 1024 * 1024

T_INPROJ = 256
T_RGLRU = 256
T_KV = 512
T_MLAPROJ = T_KV
T_Q = 2048
Q_CHAIN = 512
QK_AHEAD = 3
V_ROWS = V_DIM + 16
T_MERGE = 256
T_COMBINE = 256
ROW_CHUNKS = 8


def _params(*sem):
    return pltpu.CompilerParams(dimension_semantics=sem, vmem_limit_bytes=VMEM_LIMIT_BYTES)


def _sigmoid(z):
    return 0.5 * jnp.tanh(0.5 * z) + 0.5


def _const_spec(shape):
    n = len(shape)
    return pl.BlockSpec(shape, lambda *_: (0,) * n)


def _rows_to_natural(ref, first, rows):
    return jnp.concatenate([ref[pl.ds(first + c, rows, stride=ROW_CHUNKS), :] for c in range(ROW_CHUNKS)], axis=1)


def _natural_to_rows(ref, value):
    rows = value.shape[0]
    for c in range(ROW_CHUNKS):
        ref[pl.ds(c, rows, stride=ROW_CHUNKS), :] = value[:, c * LANES:(c + 1) * LANES]


def _inproj_body(x_ref, g_ref, w_ref, *out_refs):
    x = x_ref[...]
    u = x * lax.rsqrt(jnp.mean(x * x, axis=-1, keepdims=True) + EPS) * g_ref[...]
    u = u.astype(BF16)
    off = 0
    for ref in out_refs:
        n = ref.shape[-1]
        ref[...] = jnp.dot(u, w_ref[:, off:off + n], preferred_element_type=F32).astype(ref.dtype)
        off += n


def _in_proj(x2, g, w, widths, dtypes):
    n, d = x2.shape
    t = T_INPROJ
    return pl.pallas_call(
        _inproj_body,
        grid=(n // t,),
        in_specs=[pl.BlockSpec((t, d), lambda i: (i, 0)), _const_spec(g.shape), _const_spec(w.shape)],
        out_specs=[pl.BlockSpec((t, c), lambda i: (i, 0)) for c in widths],
        out_shape=[jax.ShapeDtypeStruct((n, c), dt) for c, dt in zip(widths, dtypes)],
        compiler_params=_params("parallel"),
        name="in_proj",
    )(x2, g, w)


def _rglru_body(xr_ref, yr_ref, ga_ref, cw_ref, cb_ref, wg_ref, ba_ref, bx_ref, lam_ref, wo_ref,
                out_ref, xbuf, a_s, b_s, h_s, hcar):
    t, c = xr_ref.shape
    nblk, bw, _ = wg_ref.shape

    @pl.when(pl.program_id(1) == 0)
    def _():
        xbuf[0:SUBLANES, :] = jnp.zeros((SUBLANES, c), F32)
        hcar[...] = jnp.zeros_like(hcar)

    xbuf[SUBLANES:, :] = xr_ref[...].astype(F32)
    lam = lam_ref[...]
    sp = jnp.maximum(-lam, 0.0) + jnp.log(1.0 + jnp.exp(-jnp.abs(lam)))
    rowmod = lax.broadcasted_iota(jnp.int32, (t // SUBLANES, SUBLANES, bw), 1)

    for n in range(nblk):
        sl = slice(n * bw, (n + 1) * bw)
        xc = cb_ref[:, sl] + cw_ref[3:4, sl] * xbuf[SUBLANES:SUBLANES + t, sl]
        for s in (1, 2, 3):
            xc = xc + cw_ref[3 - s:4 - s, sl] * xbuf[SUBLANES - s:SUBLANES - s + t, sl]
        g = jnp.dot(xc.astype(BF16), wg_ref[n], preferred_element_type=F32)
        r = _sigmoid(g[:, :bw] + ba_ref[:, sl])
        i = _sigmoid(g[:, bw:] + bx_ref[:, sl])
        log_a = -LRU_C * r * sp[:, sl]
        a = jnp.exp(log_a)
        b = xc * i * jnp.sqrt(-jnp.tanh(log_a) * (a * a + 1.0))
        a = a.reshape(t // SUBLANES, SUBLANES, bw)
        b = b.reshape(t // SUBLANES, SUBLANES, bw)
        for d in (1, 2, 4):
            keep = rowmod >= d
            a_sh = pltpu.roll(a, d, 1)
            b_sh = pltpu.roll(b, d, 1)
            b = jnp.where(keep, a * b_sh + b, b)
            a = jnp.where(keep, a * a_sh, a)
        a_s[:, sl] = a.reshape(t, bw)
        b_s[:, sl] = b.reshape(t, bw)

    xbuf[0:SUBLANES, :] = xbuf[t:t + SUBLANES, :]

    def group(gi, hb):
        rows = pl.ds(pl.multiple_of(gi * SUBLANES, SUBLANES), SUBLANES)
        h8 = a_s[rows, :] * hb + b_s[rows, :]
        h_s[rows, :] = h8
        return jnp.broadcast_to(h8[SUBLANES - 1:SUBLANES, :], (SUBLANES, c))

    hcar[...] = lax.fori_loop(0, t // SUBLANES, group, hcar[...])

    y = yr_ref[...].astype(F32)
    gelu = 0.5 * y * (1.0 + jnp.tanh(math.sqrt(2.0 / math.pi) * (y + 0.044715 * (y * y * y))))
    hg = (h_s[...] * gelu).astype(BF16)
    br = jnp.dot(hg, wo_ref[...], preferred_element_type=F32)
    out_ref[...] = (_sigmoid(ga_ref[...].astype(F32)) * br).astype(out_ref.dtype)


def _rglru(xr, yr, ga, cw, cb, wg, ba, bx, lam, wo):
    b, s, c = xr.shape
    t = T_RGLRU
    tile = pl.BlockSpec((None, t, c), lambda bi, j: (bi, j, 0))
    consts = (cw, cb, wg, ba, bx, lam, wo)
    return pl.pallas_call(
        _rglru_body,
        grid=(b, s // t),
        in_specs=[tile, tile, tile] + [_const_spec(a.shape) for a in consts],
        out_specs=tile,
        out_shape=jax.ShapeDtypeStruct((b, s, c), BF16),
        scratch_shapes=[
            pltpu.VMEM((t + SUBLANES, c), F32),
            pltpu.VMEM((t, c), F32),
            pltpu.VMEM((t, c), F32),
            pltpu.VMEM((t, c), F32),
            pltpu.VMEM((SUBLANES, c), F32),
        ],
        compiler_params=_params("parallel", "arbitrary"),
        name="rglru",
    )(xr, yr, ga, *consts)


def _mlaproj_body(cq_ref, ckv_ref, kpe_ref, posc_ref, posr_ref, qg_ref, kvg_ref, wqt_ref, wqrt_ref, wk_ref, wvt_ref,
                  gqn_ref, gqr_ref, gqrr_ref, gkn_ref, gkr_ref, gkrr_ref, frow_ref, fcol_ref,
                  qt_ref, k_ref, vt_ref):
    nh = wqt_ref.shape[0]
    t = cq_ref.shape[0]

    def latent_norm(ref, g_ref):
        z = ref[...]
        return z * lax.rsqrt(jnp.mean(z * z, axis=-1, keepdims=True) + EPS) * g_ref[...]

    cqn_t = latent_norm(cq_ref, qg_ref).T.astype(BF16)
    ckvn = latent_norm(ckv_ref, kvg_ref)
    ckvn_t = ckvn.T.astype(BF16)
    ckvn = ckvn.astype(BF16)

    ang_t = fcol_ref[...] * posr_ref[...]
    gr_cos = gqr_ref[...] * jnp.cos(ang_t)
    grr_sin = gqrr_ref[...] * jnp.sin(ang_t)
    qscale = QK_DIM ** -0.5 * math.log2(math.e)
    for h in range(nh):
        qh = jnp.dot(wqt_ref[h], cqn_t, preferred_element_type=F32)
        qr = jnp.dot(wqrt_ref[h], cqn_t, preferred_element_type=F32)
        rstd = lax.rsqrt(jnp.mean(qh * qh, axis=0, keepdims=True) + EPS) * qscale
        qt_ref[h, :QK_NOPE, :] = (qh[:QK_NOPE] * gqn_ref[...] * rstd).astype(qt_ref.dtype)
        qt_ref[h, QK_NOPE:, :] = ((qh[QK_NOPE:] * gr_cos + qr * grr_sin) * rstd).astype(qt_ref.dtype)

    ang = posc_ref[...] * frow_ref[...]
    kpe = kpe_ref[:, :QK_ROPE]
    kpr = kpe_ref[:, QK_ROPE:]
    k_rot = kpe * gkr_ref[...] * jnp.cos(ang) + kpr * gkrr_ref[...] * jnp.sin(ang)
    ss_pe = jnp.sum(kpe * kpe, axis=-1, keepdims=True)
    ones_rows = (lax.broadcasted_iota(jnp.int32, (V_ROWS - V_DIM, t), 0) == 0).astype(vt_ref.dtype)
    for h in range(nh):
        kn = jnp.dot(ckvn, wk_ref[h], preferred_element_type=F32)
        rstd = lax.rsqrt((jnp.sum(kn * kn, axis=-1, keepdims=True) + ss_pe) * (1.0 / QK_DIM) + EPS)
        k_ref[h, :, :QK_NOPE] = (kn * gkn_ref[...] * rstd).astype(k_ref.dtype)
        k_ref[h, :, QK_NOPE:] = (k_rot * rstd).astype(k_ref.dtype)
        vt_ref[h, :V_DIM, :] = jnp.dot(wvt_ref[h], ckvn_t, preferred_element_type=F32).astype(vt_ref.dtype)
        vt_ref[h, V_DIM:, :] = ones_rows


def _mla_proj(cq, ckv, kpe2, pos_col, pos_row, consts):
    b, s, _ = cq.shape
    t = T_MLAPROJ
    nh = N_HEADS

    def tile(c):
        return pl.BlockSpec((None, t, c), lambda bi, j: (bi, j, 0))

    return pl.pallas_call(
        _mlaproj_body,
        grid=(b, s // t),
        in_specs=[tile(cq.shape[-1]), tile(ckv.shape[-1]), tile(kpe2.shape[-1]), tile(1),
                  pl.BlockSpec((None, 1, t), lambda bi, j: (bi, 0, j))]
        + [_const_spec(a.shape) for a in consts],
        out_specs=[pl.BlockSpec((None, nh, QK_DIM, t), lambda bi, j: (bi, 0, 0, j)),
                   pl.BlockSpec((None, nh, t, QK_DIM), lambda bi, j: (bi, 0, j, 0)),
                   pl.BlockSpec((None, nh, None, V_ROWS, t), lambda bi, j: (bi, 0, j, 0, 0))],
        out_shape=[jax.ShapeDtypeStruct((b, nh, QK_DIM, s), BF16),
                   jax.ShapeDtypeStruct((b, nh, s, QK_DIM), BF16),
                   jax.ShapeDtypeStruct((b, nh, s // t, V_ROWS, t), BF16)],
        compiler_params=_params("parallel", "parallel"),
        name="mla_proj",
    )(cq, ckv, kpe2, pos_col, pos_row, *consts)


def _attn_body(qt_ref, k_ref, vt_ref, o_ref, m_s, acc_s):
    tq = qt_ref.shape[1]
    tk = vt_ref.shape[-1]
    dv = o_ref.shape[-1]
    nchain = tq // Q_CHAIN
    per_tile = tq // tk
    nfull = pl.program_id(2) * per_tile
    m_s[...] = jnp.full(m_s.shape, -jnp.inf, F32)
    acc_s[...] = jnp.zeros(acc_s.shape, F32)

    def scores(j, c, key_minus_query):
        kb = k_ref[pl.ds(pl.multiple_of(j * tk, tk), tk), :]
        s = jnp.dot(kb, qt_ref[:, c * Q_CHAIN:(c + 1) * Q_CHAIN], preferred_element_type=F32)
        if key_minus_query is not None:
            krow = lax.broadcasted_iota(jnp.int32, s.shape, 0) + key_minus_query
            qcol = lax.broadcasted_iota(jnp.int32, s.shape, 1)
            s = jnp.where(krow <= qcol, s, -jnp.inf)
        return s

    def softmax_pv(j, c, s):
        cs = slice(c * Q_CHAIN, (c + 1) * Q_CHAIN)
        r = s
        while r.shape[0] > SUBLANES:
            half = r.shape[0] // 2
            r = jnp.maximum(r[:half], r[half:])
        m_old = m_s[:, cs]
        m_new = jnp.maximum(m_old, jnp.max(r, axis=0, keepdims=True))
        alpha = jnp.exp2(m_old - m_new)
        p = jnp.exp2(s - m_new).astype(BF16)
        m_s[:, cs] = m_new
        acc_s[:, cs] = alpha * acc_s[:, cs] + jnp.dot(vt_ref[j], p, preferred_element_type=F32)

    def run(todo):
        pending = {i: scores(*todo[i]) for i in range(min(QK_AHEAD, len(todo)))}
        for i, (j, c, _) in enumerate(todo):
            if i + QK_AHEAD < len(todo):
                pending[i + QK_AHEAD] = scores(*todo[i + QK_AHEAD])
            softmax_pv(j, c, pending.pop(i))

    def full_blocks(jj, carry):
        run([(jj * per_tile + u, c, None) for u in range(per_tile) for c in range(nchain)])
        return carry

    lax.fori_loop(0, nfull // per_tile, full_blocks, 0)

    diagonal = []
    for u in range(per_tile):
        for c in range(nchain):
            k0, q0 = u * tk, c * Q_CHAIN
            if k0 > q0 + Q_CHAIN - 1:
                continue
            diagonal.append((nfull + u, c, None if k0 + tk - 1 <= q0 else k0 - q0))
    run(diagonal)

    out_t = acc_s[:dv, :] / acc_s[dv:dv + 1, :]
    o_ref[...] = out_t.T.astype(o_ref.dtype)


def _attention(qt, k, vt):
    b, nh, dqk, s = qt.shape
    nkb, vrows, tk = vt.shape[2:]
    tq = T_Q
    return pl.pallas_call(
        _attn_body,
        grid=(b, nh, s // tq),
        in_specs=[pl.BlockSpec((None, None, dqk, tq), lambda bi, h, i: (bi, h, 0, i)),
                  pl.BlockSpec((None, None, s, dqk), lambda bi, h, i: (bi, h, 0, 0)),
                  pl.BlockSpec((None, None, nkb, vrows, tk), lambda bi, h, i: (bi, h, 0, 0, 0))],
        out_specs=pl.BlockSpec((None, tq, V_DIM), lambda bi, h, i: (bi, i, h)),
        out_shape=jax.ShapeDtypeStruct((b, s, nh * V_DIM), BF16),
        scratch_shapes=[pltpu.VMEM((1, tq), F32), pltpu.VMEM((vrows, tq), F32)],
        compiler_params=_params("parallel", "parallel", "arbitrary"),
        name="attention",
    )(qt, k, vt)


def _merge_body(o_ref, ga_ref, gb_ref, x_ref, wmo_ref, wout_ref, g2_ref, wrh_ref, wrl_ref, rb_ref,
                hrow_ref, rec_ref, cnt_ref, cnt_s, *, n_groups, per_group):
    @pl.when(pl.program_id(0) == 0)
    def _():
        cnt_s[...] = jnp.zeros_like(cnt_s)

    br_b = jnp.dot(o_ref[...], wmo_ref[...], preferred_element_type=F32)
    merged = ga_ref[...].astype(F32) + _sigmoid(gb_ref[...].astype(F32)) * br_b
    h = x_ref[...] + jnp.dot(merged.astype(BF16), wout_ref[...], preferred_element_type=F32)
    _natural_to_rows(hrow_ref, h)

    u = h * lax.rsqrt(jnp.mean(h * h, axis=-1, keepdims=True) + EPS) * g2_ref[...]
    u_hi = u.astype(BF16)
    u_lo = (u - u_hi.astype(F32)).astype(BF16)
    logits = (jnp.dot(u_hi, wrh_ref[...], preferred_element_type=F32)
              + jnp.dot(u_lo, wrh_ref[...], preferred_element_type=F32)
              + jnp.dot(u_hi, wrl_ref[...], preferred_element_type=F32)) + rb_ref[...]

    t, nl = logits.shape
    lane = lax.broadcasted_iota(jnp.int32, (t, nl), 1)
    neg = -jnp.inf

    def first_argmax(z, zmax):
        return jnp.min(jnp.where(z == zmax, lane, nl), axis=-1, keepdims=True)

    zg = jnp.where(lane < n_groups, logits, neg)
    mg = jnp.max(zg, axis=-1, keepdims=True)
    g_sel = first_argmax(zg, mg)
    g_prob = 1.0 / jnp.sum(jnp.exp(zg - mg), axis=-1, keepdims=True)
    lo = n_groups + g_sel * per_group
    ze = jnp.where((lane >= lo) & (lane < lo + per_group), logits, neg)
    m1 = jnp.max(ze, axis=-1, keepdims=True)
    i1 = first_argmax(ze, m1)
    ze2 = jnp.where(lane == i1, neg, ze)
    m2 = jnp.max(ze2, axis=-1, keepdims=True)
    i2 = first_argmax(ze2, m2)
    e21 = jnp.exp(m2 - m1)
    w1 = g_prob / (1.0 + e21)
    w2 = w1 * e21

    e1, e2 = i1 - n_groups, i2 - n_groups
    hot = jnp.where((lane == e1) | (lane == e2), 1.0, 0.0)
    cnt_s[...] = cnt_s[...] + jnp.sum(hot, axis=0, keepdims=True)
    cnt_ref[...] = cnt_s[...]

    rec = jnp.where(lane == 0, e1.astype(F32), 0.0)
    for col, val in ((1, e2.astype(F32)), (2, w1), (3, w2)):
        rec = jnp.where(lane == col, val, rec)
    rec_ref[...] = rec


def _merge(o2, ga2, gb2, x2, wmo, wout, g2, wrh, wrl, rb, n_groups, per_group):
    n, d = x2.shape
    t = T_MERGE
    consts = (wmo, wout, g2, wrh, wrl, rb)
    tile = pl.BlockSpec((t, d), lambda i: (i, 0))
    return pl.pallas_call(
        functools.partial(_merge_body, n_groups=n_groups, per_group=per_group),
        grid=(n // t,),
        in_specs=[tile, tile, tile, tile] + [_const_spec(a.shape) for a in consts],
        out_specs=[pl.BlockSpec((t * ROW_CHUNKS, LANES), lambda i: (i, 0)),
                   pl.BlockSpec((t, ROUTER_LANES), lambda i: (i, 0)),
                   pl.BlockSpec((SUBLANES, ROUTER_LANES), lambda i: (0, 0))],
        out_shape=[jax.ShapeDtypeStruct((n * ROW_CHUNKS, LANES), F32),
                   jax.ShapeDtypeStruct((n, ROUTER_LANES), F32),
                   jax.ShapeDtypeStruct((SUBLANES, ROUTER_LANES), F32)],
        scratch_shapes=[pltpu.VMEM((SUBLANES, ROUTER_LANES), F32)],
        compiler_params=_params("arbitrary"),
        name="merge_router",
    )(o2, ga2, gb2, x2, *consts)


def _slab(ref, row):
    return ref.at[pl.ds(pl.multiple_of(row * ROW_CHUNKS, ROW_CHUNKS), ROW_CHUNKS), :]


def _experts_body(order_ref, first_ref, nvalid_ref, be_ref, nb_ref, h_hbm, g2_ref, w1_ref, w3_ref, w2_ref, y_hbm,
                  xbuf0, xbuf1, xbuf2, ybuf0, ybuf1, ybuf2, gsem, ssem):
    blk = pl.program_id(0)
    rows = MOE_BLOCK
    n_used = nb_ref[0]
    n_assign = order_ref.shape[0]
    xbufs = (xbuf0, xbuf1, xbuf2)
    ybufs = (ybuf0, ybuf1, ybuf2)
    nbuf = len(xbufs)

    def assignment(b, r):
        return order_ref[jnp.minimum(first_ref[b] + r, n_assign - 1)], r < nvalid_ref[b]

    def start_gather(b, s):
        for r in range(rows):
            a, valid = assignment(b, r)
            tok = jnp.where(valid, lax.shift_right_logical(a, TOP_K_SHIFT), 0)
            pltpu.make_async_copy(_slab(h_hbm, tok), _slab(xbufs[s], r), gsem.at[s]).start(priority=r % 2)

    def wait_gather(s):
        pltpu.make_async_copy(h_hbm.at[pl.ds(0, rows * ROW_CHUNKS), :], xbufs[s], gsem.at[s]).wait()

    def start_scatter(b, s):
        for r in range(rows):
            a, valid = assignment(b, r)
            dst = jnp.where(valid, a, n_assign + r)
            pltpu.make_async_copy(_slab(ybufs[s], r), _slab(y_hbm, dst), ssem.at[s]).start(priority=r % 2)

    def wait_scatter(s):
        pltpu.make_async_copy(ybufs[s], y_hbm.at[pl.ds(0, rows * ROW_CHUNKS), :], ssem.at[s]).wait()

    @pl.when(blk == 0)
    def _():
        ybuf1[...] = jnp.zeros_like(ybuf1)
        spare = y_hbm.at[pl.ds(y_hbm.shape[0] - rows * ROW_CHUNKS, rows * ROW_CHUNKS), :]
        fill = pltpu.make_async_copy(ybuf1, spare, ssem.at[1])
        fill.start()
        fill.wait()

    for first in range(nbuf - 1):
        @pl.when((blk == 0) & (first < n_used))
        def _():
            start_gather(first, first)

    def step(s, has_ahead):
        wait_gather(s)
        if has_ahead:
            start_gather(blk + nbuf - 1, (s + nbuf - 1) % nbuf)
        x = _rows_to_natural(xbufs[s], 0, rows)
        u = (x * lax.rsqrt(jnp.mean(x * x, axis=-1, keepdims=True) + EPS) * g2_ref[...]).astype(BF16)
        a = jnp.dot(u, w1_ref[...].astype(BF16), preferred_element_type=F32)
        c = jnp.dot(u, w3_ref[...].astype(BF16), preferred_element_type=F32)
        hmid = (a * _sigmoid(a) * c).astype(BF16)
        _natural_to_rows(ybufs[s], jnp.dot(hmid, w2_ref[...].astype(BF16), preferred_element_type=F32))

        @pl.when(blk > 0)
        def _():
            wait_scatter((s + nbuf - 1) % nbuf)
        start_scatter(blk, s)

    slot = lax.rem(blk, nbuf)
    for s in range(nbuf):
        @pl.when((slot == s) & (blk + nbuf - 1 < n_used))
        def _():
            step(s, True)

        @pl.when((slot == s) & (blk + nbuf - 1 >= n_used) & (blk < n_used))
        def _():
            step(s, False)

            @pl.when(blk + 1 == n_used)
            def _():
                wait_scatter(s)


def _experts(order, block_first, block_nvalid, block_e, n_used, h_rows, g2, w1, w3, w2):
    d = g2.shape[-1]
    rows = MOE_BLOCK
    nblk = block_e.shape[0]
    de = w1.shape[-1]
    n_out_rows = order.shape[0] + rows
    slab_buf = pltpu.VMEM((rows * ROW_CHUNKS, LANES), F32)
    grid_spec = pltpu.PrefetchScalarGridSpec(
        num_scalar_prefetch=5,
        grid=(nblk,),
        in_specs=[
            pl.BlockSpec(memory_space=pl.ANY),
            pl.BlockSpec(g2.shape, lambda i, *_: (0, 0)),
            pl.BlockSpec((None, d, de), lambda i, order, first, nvalid, be, nb: (be[i], 0, 0)),
            pl.BlockSpec((None, d, de), lambda i, order, first, nvalid, be, nb: (be[i], 0, 0)),
            pl.BlockSpec((None, de, d), lambda i, order, first, nvalid, be, nb: (be[i], 0, 0)),
        ],
        out_specs=pl.BlockSpec(memory_space=pl.ANY),
        scratch_shapes=[slab_buf] * 6 + [pltpu.SemaphoreType.DMA((3,)), pltpu.SemaphoreType.DMA((3,))],
    )
    return pl.pallas_call(
        _experts_body,
        grid_spec=grid_spec,
        out_shape=jax.ShapeDtypeStruct((n_out_rows * ROW_CHUNKS, LANES), F32),
        compiler_params=_params("arbitrary"),
        name="experts",
    )(order, block_first, block_nvalid, block_e, n_used, h_rows, g2, w1, w3, w2)


def _combine_body(hrow_ref, rec_ref, y_ref, out_ref):
    t = rec_ref.shape[0]
    acc = _rows_to_natural(hrow_ref, 0, t)
    for kk in range(TOP_K):
        gate = rec_ref[:, TOP_K + kk:TOP_K + kk + 1]
        y = jnp.concatenate([y_ref[pl.ds(kk * ROW_CHUNKS + c, t, stride=TOP_K * ROW_CHUNKS), :]
                             for c in range(ROW_CHUNKS)], axis=1)
        acc = acc + gate * y
    out_ref[...] = acc


def _combine(h_rows, rec, y_rows):
    n = rec.shape[0]
    d = ROW_CHUNKS * LANES
    t = T_COMBINE
    return pl.pallas_call(
        _combine_body,
        grid=(n // t,),
        in_specs=[pl.BlockSpec((t * ROW_CHUNKS, LANES), lambda i: (i, 0)),
                  pl.BlockSpec((t, ROUTER_LANES), lambda i: (i, 0)),
                  pl.BlockSpec((TOP_K * t * ROW_CHUNKS, LANES), lambda i: (i, 0))],
        out_specs=pl.BlockSpec((t, d), lambda i: (i, 0)),
        out_shape=jax.ShapeDtypeStruct((n, d), F32),
        compiler_params=_params("parallel"),
        name="combine",
    )(h_rows, rec, y_rows)


def _dispatch_tables(expert_id, counts):
    n_experts = counts.shape[0]
    n = expert_id.shape[0]
    m = n * TOP_K
    m_pad = m + n_experts * MOE_BLOCK
    n_blocks = m_pad // MOE_BLOCK
    flat_e = expert_id.reshape(m)
    padded = ((counts + MOE_BLOCK - 1) // MOE_BLOCK) * MOE_BLOCK
    expert = jnp.arange(n_experts, dtype=jnp.int32)
    upto = expert[None, :] <= expert[:, None]
    start = jnp.sum(jnp.where(upto, counts[None, :], 0), axis=1) - counts
    pad_end = jnp.sum(jnp.where(upto, padded[None, :], 0), axis=1)
    pad_start = pad_end - padded
    block_row = jnp.arange(n_blocks, dtype=jnp.int32) * MOE_BLOCK
    block_e = jnp.minimum(jnp.sum(pad_end[None, :] <= block_row[:, None], axis=1), n_experts - 1).astype(jnp.int32)
    n_used = (jnp.sum(padded) // MOE_BLOCK).astype(jnp.int32).reshape(1)
    mine = block_e[:, None] == expert[None, :]

    def of_block(table):
        return jnp.sum(jnp.where(mine, table[None, :], 0), axis=1)

    order = jnp.argsort(flat_e, stable=True).astype(jnp.int32)
    off = block_row - of_block(pad_start)
    block_first = (of_block(start) + off).astype(jnp.int32)
    block_nvalid = jnp.clip(of_block(counts) - off, 0, MOE_BLOCK).astype(jnp.int32)
    return order, block_first, block_nvalid, block_e, n_used


def _layer(x, positions, norm1_g, w_in, conv_w, conv_b, lru_wa, lru_ba, lru_wx, lru_bx, lru_lambda, w_rnn_o,
           q_norm_g, w_uq, kv_norm_g, w_ukv, qk_norm_q_g, qk_norm_k_g, w_mla_o, w_out, norm2_g,
           router_wg, router_bg, router_we, router_be, exp_w1, exp_w3, exp_w2):
    b, s, d = x.shape
    n = b * s
    d_rnn = conv_w.shape[-1]
    q_lora = q_norm_g.shape[-1]
    kv_lora = kv_norm_g.shape[-1]
    nh, half = N_HEADS, QK_ROPE // 2
    row = lambda a: a.reshape(1, -1).astype(F32)
    col = lambda a: a.reshape(-1, 1).astype(F32)

    o_xr, o_yr, o_cq, o_ckv, o_kpe, o_ga, o_gb = np.cumsum([0, d_rnn, d_rnn, q_lora, kv_lora, QK_ROPE, d])
    w_kpe = w_in[:, o_kpe:o_kpe + QK_ROPE]
    w_kpe_rot = jnp.concatenate([-w_kpe[:, half:], w_kpe[:, :half]], axis=1)
    w_in_k = jnp.concatenate([w_in[:, :o_kpe], w_in[:, o_ga:], w_kpe, w_kpe_rot], axis=1).astype(BF16)
    widths = (d_rnn, d_rnn, q_lora, kv_lora, d, d, 2 * QK_ROPE)
    dtypes = (BF16, BF16, F32, F32, BF16, BF16, F32)

    x2 = x.reshape(n, d)
    xr, yr, cq, ckv, ga, gb, kpe2 = _in_proj(x2, row(norm1_g), w_in_k, widths, dtypes)

    w_gate = jnp.concatenate([lru_wa, lru_wx], axis=-1).astype(BF16)
    gated_a = _rglru(xr.reshape(b, s, d_rnn), yr.reshape(b, s, d_rnn), ga.reshape(b, s, d),
                     conv_w.astype(F32), row(conv_b), w_gate, row(lru_ba), row(lru_bx), row(lru_lambda),
                     w_rnn_o.astype(BF16))

    def rot(v):
        return jnp.concatenate([-v[..., half:], v[..., :half]], axis=-1)

    def swap(v):
        return jnp.concatenate([v[..., half:], v[..., :half]], axis=-1)

    wq = w_uq.reshape(q_lora, nh, QK_DIM).transpose(1, 0, 2)
    wq_t = wq.transpose(0, 2, 1).astype(BF16)
    wqr_t = rot(wq[..., QK_NOPE:]).transpose(0, 2, 1).astype(BF16)
    wkv = w_ukv.reshape(kv_lora, nh, QK_NOPE + V_DIM).transpose(1, 0, 2)
    wk = wkv[..., :QK_NOPE].astype(BF16)
    wv_t = wkv[..., QK_NOPE:].transpose(0, 2, 1).astype(BF16)
    gq = qk_norm_q_g.astype(F32)
    gk = qk_norm_k_g.astype(F32)
    inv_freq = ROPE_THETA ** (-jnp.arange(0, QK_ROPE, 2, dtype=F32) / QK_ROPE)
    freq = jnp.concatenate([inv_freq, inv_freq])
    consts = (row(q_norm_g), row(kv_norm_g), wq_t, wqr_t, wk, wv_t,
              col(gq[:QK_NOPE]), col(gq[QK_NOPE:]), col(swap(gq[QK_NOPE:])),
              row(gk[:QK_NOPE]), row(gk[QK_NOPE:]), row(swap(gk[QK_NOPE:])), row(freq), col(freq))
    pos = positions.astype(F32)
    qt, k, vt = _mla_proj(cq.reshape(b, s, q_lora), ckv.reshape(b, s, kv_lora), kpe2.reshape(b, s, 2 * QK_ROPE),
                          pos.reshape(b, s, 1), pos.reshape(b, 1, s), consts)
    o = _attention(qt, k, vt)

    n_groups = router_wg.shape[-1]
    n_experts = router_we.shape[-1]
    per_group = n_experts // n_groups
    wr = jnp.concatenate([router_wg, router_we,
                          jnp.zeros((d, ROUTER_LANES - n_groups - n_experts), F32)], axis=1).astype(F32)
    wr_hi = wr.astype(BF16)
    wr_lo = (wr - wr_hi.astype(F32)).astype(BF16)
    rb = jnp.concatenate([router_bg, router_be,
                          jnp.zeros((ROUTER_LANES - n_groups - n_experts,), F32)]).reshape(1, ROUTER_LANES).astype(F32)
    h_rows, rec, cnt = _merge(o.reshape(n, d), gated_a.reshape(n, d), gb, x2, w_mla_o.astype(BF16),
                              w_out.astype(BF16), row(norm2_g), wr_hi, wr_lo, rb, n_groups, per_group)

    expert_id = rec[:, :TOP_K].astype(jnp.int32)
    counts = cnt[0, :n_experts].astype(jnp.int32)
    order, block_first, block_nvalid, block_e, n_used = _dispatch_tables(expert_id, counts)
    y_rows = _experts(order, block_first, block_nvalid, block_e, n_used, h_rows, row(norm2_g),
                      exp_w1, exp_w3, exp_w2)
    out = _combine(h_rows, rec, y_rows)
    return out.reshape(b, s, d)


def kernel(x, positions, norm1_g, w_in, conv_w, conv_b, lru_wa, lru_ba, lru_wx, lru_bx, lru_lambda, w_rnn_o, q_norm_g, w_uq, kv_norm_g, w_ukv, qk_norm_q_g, qk_norm_k_g, w_mla_o, w_out, norm2_g, router_wg, router_bg, router_we, router_be, exp_w1, exp_w3, exp_w2):
    args = (norm1_g, w_in, conv_w, conv_b, lru_wa, lru_ba, lru_wx, lru_bx, lru_lambda, w_rnn_o, q_norm_g, w_uq,
            kv_norm_g, w_ukv, qk_norm_q_g, qk_norm_k_g, w_mla_o, w_out, norm2_g, router_wg, router_bg, router_we,
            router_be, exp_w1, exp_w3, exp_w2)
    depth = norm1_g.shape[0]
    h = x
    for l in range(depth):
        h = _layer(h, positions, *[a[l] for a in args])
    return h
```

```python
import functools
import math

import numpy as np
import jax
import jax.numpy as jnp
from jax import lax
from jax.experimental import pallas as pl
from jax.experimental.pallas import tpu as pltpu

F32 = jnp.float32
BF16 = jnp.bfloat16
EPS = 1e-6

N_HEADS = 8
QK_NOPE = 128
QK_ROPE = 64
QK_DIM = QK_NOPE + QK_ROPE
V_DIM = 128
ROPE_THETA = 10000.0
LRU_C = 8.0
TOP_K = 2
MOE_BLOCK = 256
ROUTER_LANES = 128

SUBLANES = 8
LANES = 128
VMEM_LIMIT_BYTES = 56 * 1024 * 1024

T_INPROJ = 256
T_RGLRU = 256
T_KV = 512
T_MLAPROJ = T_KV
T_Q = 2048
Q_CHAIN = 512
QK_AHEAD = 3
V_ROWS = V_DIM + 16
T_MERGE = 256
T_COMBINE = 256
ROW_CHUNKS = 8
TOKEN_SLAB = 16


def _params(*sem):
    return pltpu.CompilerParams(dimension_semantics=sem, vmem_limit_bytes=VMEM_LIMIT_BYTES)


def _sigmoid(z):
    return 0.5 * jnp.tanh(0.5 * z) + 0.5


def _const_spec(shape):
    n = len(shape)
    return pl.BlockSpec(shape, lambda *_: (0,) * n)


def _rows_to_natural(ref, rows, slab=ROW_CHUNKS):
    return jnp.concatenate([ref[pl.ds(c, rows, stride=slab), :] for c in range(ROW_CHUNKS)], axis=1)


def _natural_to_rows(ref, value, slab=ROW_CHUNKS):
    rows = value.shape[0]
    for c in range(ROW_CHUNKS):
        ref[pl.ds(c, rows, stride=slab), :] = value[:, c * LANES:(c + 1) * LANES]


def _inproj_body(x_ref, g_ref, w_ref, *out_refs):
    x = x_ref[...]
    u = x * lax.rsqrt(jnp.mean(x * x, axis=-1, keepdims=True) + EPS) * g_ref[...]
    u = u.astype(BF16)
    off = 0
    for ref in out_refs:
        n = ref.shape[-1]
        ref[...] = jnp.dot(u, w_ref[:, off:off + n], preferred_element_type=F32).astype(ref.dtype)
        off += n


def _in_proj(x2, g, w, widths, dtypes):
    n, d = x2.shape
    t = T_INPROJ
    return pl.pallas_call(
        _inproj_body,
        grid=(n // t,),
        in_specs=[pl.BlockSpec((t, d), lambda i: (i, 0)), _const_spec(g.shape), _const_spec(w.shape)],
        out_specs=[pl.BlockSpec((t, c), lambda i: (i, 0)) for c in widths],
        out_shape=[jax.ShapeDtypeStruct((n, c), dt) for c, dt in zip(widths, dtypes)],
        compiler_params=_params("parallel"),
        name="in_proj",
    )(x2, g, w)


def _rglru_body(xr_ref, yr_ref, ga_ref, cw_ref, cb_ref, wg_ref, ba_ref, bx_ref, lam_ref, wo_ref,
                out_ref, xbuf, a_s, b_s, h_s, hcar):
    t, c = xr_ref.shape
    nblk, bw, _ = wg_ref.shape

    @pl.when(pl.program_id(1) == 0)
    def _():
        xbuf[0:SUBLANES, :] = jnp.zeros((SUBLANES, c), F32)
        hcar[...] = jnp.zeros_like(hcar)

    xbuf[SUBLANES:, :] = xr_ref[...].astype(F32)
    lam = lam_ref[...]
    sp = jnp.maximum(-lam, 0.0) + jnp.log(1.0 + jnp.exp(-jnp.abs(lam)))
    rowmod = lax.broadcasted_iota(jnp.int32, (t // SUBLANES, SUBLANES, bw), 1)

    for n in range(nblk):
        sl = slice(n * bw, (n + 1) * bw)
        xc = cb_ref[:, sl] + cw_ref[3:4, sl] * xbuf[SUBLANES:SUBLANES + t, sl]
        for s in (1, 2, 3):
            xc = xc + cw_ref[3 - s:4 - s, sl] * xbuf[SUBLANES - s:SUBLANES - s + t, sl]
        g = jnp.dot(xc.astype(BF16), wg_ref[n], preferred_element_type=F32)
        r = _sigmoid(g[:, :bw] + ba_ref[:, sl])
        i = _sigmoid(g[:, bw:] + bx_ref[:, sl])
        log_a = -LRU_C * r * sp[:, sl]
        a = jnp.exp(log_a)
        b = xc * i * jnp.sqrt(-jnp.tanh(log_a) * (a * a + 1.0))
        a = a.reshape(t // SUBLANES, SUBLANES, bw)
        b = b.reshape(t // SUBLANES, SUBLANES, bw)
        for d in (1, 2, 4):
            keep = rowmod >= d
            a_sh = pltpu.roll(a, d, 1)
            b_sh = pltpu.roll(b, d, 1)
            b = jnp.where(keep, a * b_sh + b, b)
            a = jnp.where(keep, a * a_sh, a)
        a_s[:, sl] = a.reshape(t, bw)
        b_s[:, sl] = b.reshape(t, bw)

    xbuf[0:SUBLANES, :] = xbuf[t:t + SUBLANES, :]

    def group(gi, hb):
        rows = pl.ds(pl.multiple_of(gi * SUBLANES, SUBLANES), SUBLANES)
        h8 = a_s[rows, :] * hb + b_s[rows, :]
        h_s[rows, :] = h8
        return jnp.broadcast_to(h8[SUBLANES - 1:SUBLANES, :], (SUBLANES, c))

    hcar[...] = lax.fori_loop(0, t // SUBLANES, group, hcar[...])

    y = yr_ref[...].astype(F32)
    gelu = 0.5 * y * (1.0 + jnp.tanh(math.sqrt(2.0 / math.pi) * (y + 0.044715 * (y * y * y))))
    hg = (h_s[...] * gelu).astype(BF16)
    br = jnp.dot(hg, wo_ref[...], preferred_element_type=F32)
    out_ref[...] = (_sigmoid(ga_ref[...].astype(F32)) * br).astype(out_ref.dtype)


def _rglru(xr, yr, ga, cw, cb, wg, ba, bx, lam, wo):
    b, s, c = xr.shape
    t = T_RGLRU
    tile = pl.BlockSpec((None, t, c), lambda bi, j: (bi, j, 0))
    consts = (cw, cb, wg, ba, bx, lam, wo)
    return pl.pallas_call(
        _rglru_body,
        grid=(b, s // t),
        in_specs=[tile, tile, tile] + [_const_spec(a.shape) for a in consts],
        out_specs=tile,
        out_shape=jax.ShapeDtypeStruct((b, s, c), BF16),
        scratch_shapes=[
            pltpu.VMEM((t + SUBLANES, c), F32),
            pltpu.VMEM((t, c), F32),
            pltpu.VMEM((t, c), F32),
            pltpu.VMEM((t, c), F32),
            pltpu.VMEM((SUBLANES, c), F32),
        ],
        compiler_params=_params("parallel", "arbitrary"),
        name="rglru",
    )(xr, yr, ga, *consts)


def _mlaproj_body(cq_ref, ckv_ref, kpe_ref, posc_ref, posr_ref, qg_ref, kvg_ref, wqt_ref, wqrt_ref, wk_ref, wvt_ref,
                  gqn_ref, gqr_ref, gqrr_ref, gkn_ref, gkr_ref, gkrr_ref, frow_ref, fcol_ref,
                  qt_ref, k_ref, vt_ref):
    nh = wqt_ref.shape[0]
    t = cq_ref.shape[0]

    def latent_norm(ref, g_ref):
        z = ref[...]
        return z * lax.rsqrt(jnp.mean(z * z, axis=-1, keepdims=True) + EPS) * g_ref[...]

    cqn_t = latent_norm(cq_ref, qg_ref).T.astype(BF16)
    ckvn = latent_norm(ckv_ref, kvg_ref)
    ckvn_t = ckvn.T.astype(BF16)
    ckvn = ckvn.astype(BF16)

    ang_t = fcol_ref[...] * posr_ref[...]
    gr_cos = gqr_ref[...] * jnp.cos(ang_t)
    grr_sin = gqrr_ref[...] * jnp.sin(ang_t)
    qscale = QK_DIM ** -0.5 * math.log2(math.e)
    for h in range(nh):
        qh = jnp.dot(wqt_ref[h], cqn_t, preferred_element_type=F32)
        qr = jnp.dot(wqrt_ref[h], cqn_t, preferred_element_type=F32)
        rstd = lax.rsqrt(jnp.mean(qh * qh, axis=0, keepdims=True) + EPS) * qscale
        qt_ref[h, :QK_NOPE, :] = (qh[:QK_NOPE] * gqn_ref[...] * rstd).astype(qt_ref.dtype)
        qt_ref[h, QK_NOPE:, :] = ((qh[QK_NOPE:] * gr_cos + qr * grr_sin) * rstd).astype(qt_ref.dtype)

    ang = posc_ref[...] * frow_ref[...]
    kpe = kpe_ref[:, :QK_ROPE]
    kpr = kpe_ref[:, QK_ROPE:]
    k_rot = kpe * gkr_ref[...] * jnp.cos(ang) + kpr * gkrr_ref[...] * jnp.sin(ang)
    ss_pe = jnp.sum(kpe * kpe, axis=-1, keepdims=True)
    ones_rows = (lax.broadcasted_iota(jnp.int32, (V_ROWS - V_DIM, t), 0) == 0).astype(vt_ref.dtype)
    for h in range(nh):
        kn = jnp.dot(ckvn, wk_ref[h], preferred_element_type=F32)
        rstd = lax.rsqrt((jnp.sum(kn * kn, axis=-1, keepdims=True) + ss_pe) * (1.0 / QK_DIM) + EPS)
        k_ref[h, :, :QK_NOPE] = (kn * gkn_ref[...] * rstd).astype(k_ref.dtype)
        k_ref[h, :, QK_NOPE:] = (k_rot * rstd).astype(k_ref.dtype)
        vt_ref[h, :V_DIM, :] = jnp.dot(wvt_ref[h], ckvn_t, preferred_element_type=F32).astype(vt_ref.dtype)
        vt_ref[h, V_DIM:, :] = ones_rows


def _mla_proj(cq, ckv, kpe2, pos_col, pos_row, consts):
    b, s, _ = cq.shape
    t = T_MLAPROJ
    nh = N_HEADS

    def tile(c):
        return pl.BlockSpec((None, t, c), lambda bi, j: (bi, j, 0))

    return pl.pallas_call(
        _mlaproj_body,
        grid=(b, s // t),
        in_specs=[tile(cq.shape[-1]), tile(ckv.shape[-1]), tile(kpe2.shape[-1]), tile(1),
                  pl.BlockSpec((None, 1, t), lambda bi, j: (bi, 0, j))]
        + [_const_spec(a.shape) for a in consts],
        out_specs=[pl.BlockSpec((None, nh, QK_DIM, t), lambda bi, j: (bi, 0, 0, j)),
                   pl.BlockSpec((None, nh, t, QK_DIM), lambda bi, j: (bi, 0, j, 0)),
                   pl.BlockSpec((None, nh, None, V_ROWS, t), lambda bi, j: (bi, 0, j, 0, 0))],
        out_shape=[jax.ShapeDtypeStruct((b, nh, QK_DIM, s), BF16),
                   jax.ShapeDtypeStruct((b, nh, s, QK_DIM), BF16),
                   jax.ShapeDtypeStruct((b, nh, s // t, V_ROWS, t), BF16)],
        compiler_params=_params("parallel", "parallel"),
        name="mla_proj",
    )(cq, ckv, kpe2, pos_col, pos_row, *consts)


def _attn_body(qt_ref, k_ref, vt_ref, o_ref, m_s, acc_s):
    tq = qt_ref.shape[1]
    tk = vt_ref.shape[-1]
    dv = o_ref.shape[-1]
    nchain = tq // Q_CHAIN
    per_tile = tq // tk
    nfull = pl.program_id(2) * per_tile
    m_s[...] = jnp.full(m_s.shape, -jnp.inf, F32)
    acc_s[...] = jnp.zeros(acc_s.shape, F32)

    def scores(j, c, key_minus_query):
        kb = k_ref[pl.ds(pl.multiple_of(j * tk, tk), tk), :]
        s = jnp.dot(kb, qt_ref[:, c * Q_CHAIN:(c + 1) * Q_CHAIN], preferred_element_type=F32)
        if key_minus_query is not None:
            krow = lax.broadcasted_iota(jnp.int32, s.shape, 0) + key_minus_query
            qcol = lax.broadcasted_iota(jnp.int32, s.shape, 1)
            s = jnp.where(krow <= qcol, s, -jnp.inf)
        return s

    def softmax_pv(j, c, s):
        cs = slice(c * Q_CHAIN, (c + 1) * Q_CHAIN)
        r = s
        while r.shape[0] > SUBLANES:
            half = r.shape[0] // 2
            r = jnp.maximum(r[:half], r[half:])
        m_old = m_s[:, cs]
        m_new = jnp.maximum(m_old, jnp.max(r, axis=0, keepdims=True))
        alpha = jnp.exp2(m_old - m_new)
        p = jnp.exp2(s - m_new).astype(BF16)
        m_s[:, cs] = m_new
        acc_s[:, cs] = alpha * acc_s[:, cs] + jnp.dot(vt_ref[j], p, preferred_element_type=F32)

    def run(todo):
        pending = {i: scores(*todo[i]) for i in range(min(QK_AHEAD, len(todo)))}
        for i, (j, c, _) in enumerate(todo):
            if i + QK_AHEAD < len(todo):
                pending[i + QK_AHEAD] = scores(*todo[i + QK_AHEAD])
            softmax_pv(j, c, pending.pop(i))

    def full_blocks(jj, carry):
        run([(jj * per_tile + u, c, None) for u in range(per_tile) for c in range(nchain)])
        return carry

    lax.fori_loop(0, nfull // per_tile, full_blocks, 0)

    diagonal = []
    for u in range(per_tile):
        for c in range(nchain):
            k0, q0 = u * tk, c * Q_CHAIN
            if k0 > q0 + Q_CHAIN - 1:
                continue
            diagonal.append((nfull + u, c, None if k0 + tk - 1 <= q0 else k0 - q0))
    run(diagonal)

    out_t = acc_s[:dv, :] / acc_s[dv:dv + 1, :]
    o_ref[...] = out_t.T.astype(o_ref.dtype)


def _attention(qt, k, vt):
    b, nh, dqk, s = qt.shape
    nkb, vrows, tk = vt.shape[2:]
    tq = T_Q
    return pl.pallas_call(
        _attn_body,
        grid=(b, nh, s // tq),
        in_specs=[pl.BlockSpec((None, None, dqk, tq), lambda bi, h, i: (bi, h, 0, i)),
                  pl.BlockSpec((None, None, s, dqk), lambda bi, h, i: (bi, h, 0, 0)),
                  pl.BlockSpec((None, None, nkb, vrows, tk), lambda bi, h, i: (bi, h, 0, 0, 0))],
        out_specs=pl.BlockSpec((None, tq, V_DIM), lambda bi, h, i: (bi, i, h)),
        out_shape=jax.ShapeDtypeStruct((b, s, nh * V_DIM), BF16),
        scratch_shapes=[pltpu.VMEM((1, tq), F32), pltpu.VMEM((vrows, tq), F32)],
        compiler_params=_params("parallel", "parallel", "arbitrary"),
        name="attention",
    )(qt, k, vt)


def _merge_body(o_ref, ga_ref, gb_ref, x_ref, wmo_ref, wout_ref, g2_ref, wrh_ref, wrl_ref, rb_ref,
                hrow_ref, rec_ref, cnt_ref, cnt_s, *, n_groups, per_group):
    @pl.when(pl.program_id(0) == 0)
    def _():
        cnt_s[...] = jnp.zeros_like(cnt_s)

    br_b = jnp.dot(o_ref[...], wmo_ref[...], preferred_element_type=F32)
    merged = ga_ref[...].astype(F32) + _sigmoid(gb_ref[...].astype(F32)) * br_b
    h = x_ref[...] + jnp.dot(merged.astype(BF16), wout_ref[...], preferred_element_type=F32)

    u = h * lax.rsqrt(jnp.mean(h * h, axis=-1, keepdims=True) + EPS) * g2_ref[...]
    u_hi = u.astype(BF16)
    u_lo = (u - u_hi.astype(F32)).astype(BF16)
    logits = (jnp.dot(u_hi, wrh_ref[...], preferred_element_type=F32)
              + jnp.dot(u_lo, wrh_ref[...], preferred_element_type=F32)
              + jnp.dot(u_hi, wrl_ref[...], preferred_element_type=F32)) + rb_ref[...]

    t, nl = logits.shape
    lane = lax.broadcasted_iota(jnp.int32, (t, nl), 1)
    neg = -jnp.inf

    def first_argmax(z, zmax):
        return jnp.min(jnp.where(z == zmax, lane, nl), axis=-1, keepdims=True)

    zg = jnp.where(lane < n_groups, logits, neg)
    mg = jnp.max(zg, axis=-1, keepdims=True)
    g_sel = first_argmax(zg, mg)
    g_prob = 1.0 / jnp.sum(jnp.exp(zg - mg), axis=-1, keepdims=True)
    lo = n_groups + g_sel * per_group
    ze = jnp.where((lane >= lo) & (lane < lo + per_group), logits, neg)
    m1 = jnp.max(ze, axis=-1, keepdims=True)
    i1 = first_argmax(ze, m1)
    ze2 = jnp.where(lane == i1, neg, ze)
    m2 = jnp.max(ze2, axis=-1, keepdims=True)
    i2 = first_argmax(ze2, m2)
    e21 = jnp.exp(m2 - m1)
    w1 = g_prob / (1.0 + e21)
    w2 = w1 * e21

    cnt_s[...] = cnt_s[...] + jnp.sum(jnp.where(lane == g_sel, 1.0, 0.0), axis=0, keepdims=True)
    cnt_ref[...] = cnt_s[...]

    rec = jnp.where(lane == 0, (i1 - lo).astype(F32), 0.0)
    for col, val in ((1, (i2 - lo).astype(F32)), (2, w1), (3, w2), (4, g_sel.astype(F32))):
        rec = jnp.where(lane == col, val, rec)
    rec_ref[...] = rec

    _natural_to_rows(hrow_ref, h, TOKEN_SLAB)
    hrow_ref[pl.ds(ROW_CHUNKS, t, stride=TOKEN_SLAB), :] = rec
    for c in range(ROW_CHUNKS + 1, TOKEN_SLAB):
        hrow_ref[pl.ds(c, t, stride=TOKEN_SLAB), :] = jnp.zeros((t, nl), F32)


def _merge(o2, ga2, gb2, x2, wmo, wout, g2, wrh, wrl, rb, n_groups, per_group):
    n, d = x2.shape
    t = T_MERGE
    consts = (wmo, wout, g2, wrh, wrl, rb)
    tile = pl.BlockSpec((t, d), lambda i: (i, 0))
    return pl.pallas_call(
        functools.partial(_merge_body, n_groups=n_groups, per_group=per_group),
        grid=(n // t,),
        in_specs=[tile, tile, tile, tile] + [_const_spec(a.shape) for a in consts],
        out_specs=[pl.BlockSpec((t * TOKEN_SLAB, LANES), lambda i: (i, 0)),
                   pl.BlockSpec((t, ROUTER_LANES), lambda i: (i, 0)),
                   pl.BlockSpec((SUBLANES, ROUTER_LANES), lambda i: (0, 0))],
        out_shape=[jax.ShapeDtypeStruct((n * TOKEN_SLAB, LANES), F32),
                   jax.ShapeDtypeStruct((n, ROUTER_LANES), F32),
                   jax.ShapeDtypeStruct((SUBLANES, ROUTER_LANES), F32)],
        scratch_shapes=[pltpu.VMEM((SUBLANES, ROUTER_LANES), F32)],
        compiler_params=_params("arbitrary"),
        name="merge_router",
    )(o2, ga2, gb2, x2, *consts)


def _slab(ref, row, slab=ROW_CHUNKS):
    return ref.at[pl.ds(pl.multiple_of(row * slab, slab), slab), :]


def _experts_body(order_ref, first_ref, nvalid_ref, bg_ref, nb_ref, h_hbm, g2_ref, w1_ref, w3_ref, w2_ref, y_hbm,
                  xbuf0, xbuf1, xbuf2, ybuf0, ybuf1, ybuf2, gsem, ssem):
    blk = pl.program_id(0)
    rows = MOE_BLOCK
    n_used = nb_ref[0]
    n_tok = order_ref.shape[0]
    per_group = w1_ref.shape[0]
    xbufs = (xbuf0, xbuf1, xbuf2)
    ybufs = (ybuf0, ybuf1, ybuf2)
    nbuf = len(xbufs)

    def token(b, r):
        return order_ref[jnp.minimum(first_ref[b] + r, n_tok - 1)], r < nvalid_ref[b]

    def start_gather(b, s):
        for r in range(rows):
            tok, valid = token(b, r)
            tok = jnp.where(valid, tok, 0)
            pltpu.make_async_copy(_slab(h_hbm, tok, TOKEN_SLAB), _slab(xbufs[s], r, TOKEN_SLAB),
                                  gsem.at[s]).start(priority=r % 2)

    def wait_gather(s):
        pltpu.make_async_copy(h_hbm.at[pl.ds(0, rows * TOKEN_SLAB), :], xbufs[s], gsem.at[s]).wait()

    def start_scatter(b, s):
        for r in range(rows):
            tok, valid = token(b, r)
            dst = jnp.where(valid, tok, n_tok + r)
            pltpu.make_async_copy(_slab(ybufs[s], r), _slab(y_hbm, dst), ssem.at[s]).start(priority=r % 2)

    def wait_scatter(s):
        pltpu.make_async_copy(ybufs[s], y_hbm.at[pl.ds(0, rows * ROW_CHUNKS), :], ssem.at[s]).wait()

    @pl.when(blk == 0)
    def _():
        ybuf1[...] = jnp.zeros_like(ybuf1)
        spare = y_hbm.at[pl.ds(y_hbm.shape[0] - rows * ROW_CHUNKS, rows * ROW_CHUNKS), :]
        fill = pltpu.make_async_copy(ybuf1, spare, ssem.at[1])
        fill.start()
        fill.wait()

    for first in range(nbuf - 1):
        @pl.when((blk == 0) & (first < n_used))
        def _():
            start_gather(first, first)

    def step(s, has_ahead):
        wait_gather(s)
        if has_ahead:
            start_gather(blk + nbuf - 1, (s + nbuf - 1) % nbuf)
        x = _rows_to_natural(xbufs[s], rows, TOKEN_SLAB)
        rec = xbufs[s][pl.ds(ROW_CHUNKS, rows, stride=TOKEN_SLAB), :]
        pick = [rec[:, kk:kk + 1] for kk in range(TOP_K)]
        gate = [rec[:, TOP_K + kk:TOP_K + kk + 1] for kk in range(TOP_K)]
        u = (x * lax.rsqrt(jnp.mean(x * x, axis=-1, keepdims=True) + EPS) * g2_ref[...]).astype(BF16)
        y = x
        for j in range(per_group):
            a = jnp.dot(u, w1_ref[j], preferred_element_type=F32)
            c = jnp.dot(u, w3_ref[j], preferred_element_type=F32)
            g = jnp.zeros_like(gate[0])
            for kk in range(TOP_K):
                g = jnp.where(pick[kk] == float(j), gate[kk], g)
            routed = functools.reduce(jnp.logical_or, [pick[kk] == float(j) for kk in range(TOP_K)])
            hmid = jnp.where(routed, a * _sigmoid(a) * c * g, 0.0).astype(BF16)
            y = y + jnp.dot(hmid, w2_ref[j], preferred_element_type=F32)
        _natural_to_rows(ybufs[s], y)

        @pl.when(blk > 0)
        def _():
            wait_scatter((s + nbuf - 1) % nbuf)
        start_scatter(blk, s)

    slot = lax.rem(blk, nbuf)
    for s in range(nbuf):
        @pl.when((slot == s) & (blk + nbuf - 1 < n_used))
        def _():
            step(s, True)

        @pl.when((slot == s) & (blk + nbuf - 1 >= n_used) & (blk < n_used))
        def _():
            step(s, False)

            @pl.when(blk + 1 == n_used)
            def _():
                wait_scatter(s)


def _experts(order, block_first, block_nvalid, block_g, n_used, h_slabs, g2, w1, w3, w2):
    d = g2.shape[-1]
    rows = MOE_BLOCK
    nblk = block_g.shape[0]
    per_group, de = w1.shape[1], w1.shape[-1]
    n_out_rows = order.shape[0] + rows
    grid_spec = pltpu.PrefetchScalarGridSpec(
        num_scalar_prefetch=5,
        grid=(nblk,),
        in_specs=[
            pl.BlockSpec(memory_space=pl.ANY),
            pl.BlockSpec(g2.shape, lambda i, *_: (0, 0)),
            pl.BlockSpec((None, per_group, d, de), lambda i, order, first, nvalid, bg, nb: (bg[i], 0, 0, 0)),
            pl.BlockSpec((None, per_group, d, de), lambda i, order, first, nvalid, bg, nb: (bg[i], 0, 0, 0)),
            pl.BlockSpec((None, per_group, de, d), lambda i, order, first, nvalid, bg, nb: (bg[i], 0, 0, 0)),
        ],
        out_specs=pl.BlockSpec(memory_space=pl.ANY),
        scratch_shapes=[pltpu.VMEM((rows * TOKEN_SLAB, LANES), F32)] * 3
        + [pltpu.VMEM((rows * ROW_CHUNKS, LANES), F32)] * 3
        + [pltpu.SemaphoreType.DMA((3,)), pltpu.SemaphoreType.DMA((3,))],
    )
    return pl.pallas_call(
        _experts_body,
        grid_spec=grid_spec,
        out_shape=jax.ShapeDtypeStruct((n_out_rows * ROW_CHUNKS, LANES), F32),
        compiler_params=_params("arbitrary"),
        name="experts",
    )(order, block_first, block_nvalid, block_g, n_used, h_slabs, g2, w1, w3, w2)


def _unrow_body(rows_ref, out_ref):
    out_ref[...] = _rows_to_natural(rows_ref, out_ref.shape[0])


def _unrow(out_rows, n):
    d = ROW_CHUNKS * LANES
    t = T_COMBINE
    return pl.pallas_call(
        _unrow_body,
        grid=(n // t,),
        in_specs=[pl.BlockSpec((t * ROW_CHUNKS, LANES), lambda i: (i, 0))],
        out_specs=pl.BlockSpec((t, d), lambda i: (i, 0)),
        out_shape=jax.ShapeDtypeStruct((n, d), F32),
        compiler_params=_params("parallel"),
        name="unrow",
    )(out_rows)


def _dispatch_tables(flat_e, counts):
    n_experts = counts.shape[0]
    m = flat_e.shape[0]
    m_pad = m + n_experts * MOE_BLOCK
    n_blocks = m_pad // MOE_BLOCK
    padded = ((counts + MOE_BLOCK - 1) // MOE_BLOCK) * MOE_BLOCK
    expert = jnp.arange(n_experts, dtype=jnp.int32)
    upto = expert[None, :] <= expert[:, None]
    start = jnp.sum(jnp.where(upto, counts[None, :], 0), axis=1) - counts
    pad_end = jnp.sum(jnp.where(upto, padded[None, :], 0), axis=1)
    pad_start = pad_end - padded
    block_row = jnp.arange(n_blocks, dtype=jnp.int32) * MOE_BLOCK
    block_e = jnp.minimum(jnp.sum(pad_end[None, :] <= block_row[:, None], axis=1), n_experts - 1).astype(jnp.int32)
    n_used = (jnp.sum(padded) // MOE_BLOCK).astype(jnp.int32).reshape(1)
    mine = block_e[:, None] == expert[None, :]

    def of_block(table):
        return jnp.sum(jnp.where(mine, table[None, :], 0), axis=1)

    order = jnp.argsort(flat_e, stable=True).astype(jnp.int32)
    off = block_row - of_block(pad_start)
    block_first = (of_block(start) + off).astype(jnp.int32)
    block_nvalid = jnp.clip(of_block(counts) - off, 0, MOE_BLOCK).astype(jnp.int32)
    return order, block_first, block_nvalid, block_e, n_used


def _layer(x, positions, norm1_g, w_in, conv_w, conv_b, lru_wa, lru_ba, lru_wx, lru_bx, lru_lambda, w_rnn_o,
           q_norm_g, w_uq, kv_norm_g, w_ukv, qk_norm_q_g, qk_norm_k_g, w_mla_o, w_out, norm2_g,
           router_wg, router_bg, router_we, router_be, exp_w1, exp_w3, exp_w2):
    b, s, d = x.shape
    n = b * s
    d_rnn = conv_w.shape[-1]
    q_lora = q_norm_g.shape[-1]
    kv_lora = kv_norm_g.shape[-1]
    nh, half = N_HEADS, QK_ROPE // 2
    row = lambda a: a.reshape(1, -1).astype(F32)
    col = lambda a: a.reshape(-1, 1).astype(F32)

    o_xr, o_yr, o_cq, o_ckv, o_kpe, o_ga, o_gb = np.cumsum([0, d_rnn, d_rnn, q_lora, kv_lora, QK_ROPE, d])
    w_kpe = w_in[:, o_kpe:o_kpe + QK_ROPE]
    w_kpe_rot = jnp.concatenate([-w_kpe[:, half:], w_kpe[:, :half]], axis=1)
    w_in_k = jnp.concatenate([w_in[:, :o_kpe], w_in[:, o_ga:], w_kpe, w_kpe_rot], axis=1).astype(BF16)
    widths = (d_rnn, d_rnn, q_lora, kv_lora, d, d, 2 * QK_ROPE)
    dtypes = (BF16, BF16, F32, F32, BF16, BF16, F32)

    x2 = x.reshape(n, d)
    xr, yr, cq, ckv, ga, gb, kpe2 = _in_proj(x2, row(norm1_g), w_in_k, widths, dtypes)

    w_gate = jnp.concatenate([lru_wa, lru_wx], axis=-1).astype(BF16)
    gated_a = _rglru(xr.reshape(b, s, d_rnn), yr.reshape(b, s, d_rnn), ga.reshape(b, s, d),
                     conv_w.astype(F32), row(conv_b), w_gate, row(lru_ba), row(lru_bx), row(lru_lambda),
                     w_rnn_o.astype(BF16))

    def rot(v):
        return jnp.concatenate([-v[..., half:], v[..., :half]], axis=-1)

    def swap(v):
        return jnp.concatenate([v[..., half:], v[..., :half]], axis=-1)

    wq = w_uq.reshape(q_lora, nh, QK_DIM).transpose(1, 0, 2)
    wq_t = wq.transpose(0, 2, 1).astype(BF16)
    wqr_t = rot(wq[..., QK_NOPE:]).transpose(0, 2, 1).astype(BF16)
    wkv = w_ukv.reshape(kv_lora, nh, QK_NOPE + V_DIM).transpose(1, 0, 2)
    wk = wkv[..., :QK_NOPE].astype(BF16)
    wv_t = wkv[..., QK_NOPE:].transpose(0, 2, 1).astype(BF16)
    gq = qk_norm_q_g.astype(F32)
    gk = qk_norm_k_g.astype(F32)
    inv_freq = ROPE_THETA ** (-jnp.arange(0, QK_ROPE, 2, dtype=F32) / QK_ROPE)
    freq = jnp.concatenate([inv_freq, inv_freq])
    consts = (row(q_norm_g), row(kv_norm_g), wq_t, wqr_t, wk, wv_t,
              col(gq[:QK_NOPE]), col(gq[QK_NOPE:]), col(swap(gq[QK_NOPE:])),
              row(gk[:QK_NOPE]), row(gk[QK_NOPE:]), row(swap(gk[QK_NOPE:])), row(freq), col(freq))
    pos = positions.astype(F32)
    qt, k, vt = _mla_proj(cq.reshape(b, s, q_lora), ckv.reshape(b, s, kv_lora), kpe2.reshape(b, s, 2 * QK_ROPE),
                          pos.reshape(b, s, 1), pos.reshape(b, 1, s), consts)
    o = _attention(qt, k, vt)

    n_groups = router_wg.shape[-1]
    n_experts = router_we.shape[-1]
    per_group = n_experts // n_groups
    wr = jnp.concatenate([router_wg, router_we,
                          jnp.zeros((d, ROUTER_LANES - n_groups - n_experts), F32)], axis=1).astype(F32)
    wr_hi = wr.astype(BF16)
    wr_lo = (wr - wr_hi.astype(F32)).astype(BF16)
    rb = jnp.concatenate([router_bg, router_be,
                          jnp.zeros((ROUTER_LANES - n_groups - n_experts,), F32)]).reshape(1, ROUTER_LANES).astype(F32)
    h_slabs, rec, cnt = _merge(o.reshape(n, d), gated_a.reshape(n, d), gb, x2, w_mla_o.astype(BF16),
                               w_out.astype(BF16), row(norm2_g), wr_hi, wr_lo, rb, n_groups, per_group)

    group_id = rec[:, 2 * TOP_K].astype(jnp.int32)
    counts = cnt[0, :n_groups].astype(jnp.int32)
    order, block_first, block_nvalid, block_g, n_used = _dispatch_tables(group_id, counts)
    de = exp_w1.shape[-1]
    out_rows = _experts(order, block_first, block_nvalid, block_g, n_used, h_slabs, row(norm2_g),
                        exp_w1.astype(BF16).reshape(n_groups, per_group, d, de),
                        exp_w3.astype(BF16).reshape(n_groups, per_group, d, de),
                        exp_w2.astype(BF16).reshape(n_groups, per_group, de, d))
    return _unrow(out_rows, n).reshape(b, s, d)


def kernel(x, positions, norm1_g, w_in, conv_w, conv_b, lru_wa, lru_ba, lru_wx, lru_bx, lru_lambda, w_rnn_o, q_norm_g, w_uq, kv_norm_g, w_ukv, qk_norm_q_g, qk_norm_k_g, w_mla_o, w_out, norm2_g, router_wg, router_bg, router_we, router_be, exp_w1, exp_w3, exp_w2):
    args = (norm1_g, w_in, conv_w, conv_b, lru_wa, lru_ba, lru_wx, lru_bx, lru_lambda, w_rnn_o, q_norm_g, w_uq,
            kv_norm_g, w_ukv, qk_norm_q_g, qk_norm_k_g, w_mla_o, w_out, norm2_g, router_wg, router_bg, router_we,
            router_be, exp_w1, exp_w3, exp_w2)
    depth = norm1_g.shape[0]
    h = x
    for l in range(depth):
        h = _layer(h, positions, *[a[l] for a in args])
    return h
```

```python
import functools
import math

import numpy as np
import jax
import jax.numpy as jnp
from jax import lax
from jax.experimental import pallas as pl
from jax.experimental.pallas import tpu as pltpu

F32 = jnp.float32
BF16 = jnp.bfloat16
EPS = 1e-6

N_HEADS = 8
QK_NOPE = 128
QK_ROPE = 64
QK_DIM = QK_NOPE + QK_ROPE
V_DIM = 128
ROPE_THETA = 10000.0
LRU_C = 8.0
TOP_K = 2
MOE_BLOCK = 256
ROUTER_LANES = 128

SUBLANES = 8
LANES = 128
VMEM_LIMIT_BYTES = 56 * 1024 * 1024

T_INPROJ = 256
T_RGLRU = 256
T_KV = 512
T_MLAPROJ = T_KV
T_Q = 2048
Q_CHAIN = 512
QK_AHEAD = 3
V_ROWS = V_DIM + 16
T_MERGE = 256
T_COMBINE = 512
ROW_CHUNKS = 8
TOKEN_SLAB = 16


def _params(*sem):
    return pltpu.CompilerParams(dimension_semantics=sem, vmem_limit_bytes=VMEM_LIMIT_BYTES)


def _sigmoid(z):
    return 0.5 * jnp.tanh(0.5 * z) + 0.5


def _const_spec(shape):
    n = len(shape)
    return pl.BlockSpec(shape, lambda *_: (0,) * n)


def _rows_to_natural(ref, rows, slab=ROW_CHUNKS):
    return jnp.concatenate([ref[pl.ds(c, rows, stride=slab), :] for c in range(ROW_CHUNKS)], axis=1)


def _natural_to_rows(ref, value, slab=ROW_CHUNKS):
    rows = value.shape[0]
    for c in range(ROW_CHUNKS):
        ref[pl.ds(c, rows, stride=slab), :] = value[:, c * LANES:(c + 1) * LANES]


def _inproj_body(x_ref, g_ref, w_ref, *out_refs):
    x = x_ref[...]
    u = x * lax.rsqrt(jnp.mean(x * x, axis=-1, keepdims=True) + EPS) * g_ref[...]
    u = u.astype(BF16)
    off = 0
    for ref in out_refs:
        n = ref.shape[-1]
        ref[...] = jnp.dot(u, w_ref[:, off:off + n], preferred_element_type=F32).astype(ref.dtype)
        off += n


def _in_proj(x2, g, w, widths, dtypes):
    n, d = x2.shape
    t = T_INPROJ
    return pl.pallas_call(
        _inproj_body,
        grid=(n // t,),
        in_specs=[pl.BlockSpec((t, d), lambda i: (i, 0)), _const_spec(g.shape), _const_spec(w.shape)],
        out_specs=[pl.BlockSpec((t, c), lambda i: (i, 0)) for c in widths],
        out_shape=[jax.ShapeDtypeStruct((n, c), dt) for c, dt in zip(widths, dtypes)],
        compiler_params=_params("parallel"),
        name="in_proj",
    )(x2, g, w)


def _rglru_body(xr_ref, yr_ref, ga_ref, cw_ref, cb_ref, wg_ref, ba_ref, bx_ref, lam_ref, wo_ref,
                out_ref, xbuf, a_s, b_s, h_s, hcar):
    t, c = xr_ref.shape
    nblk, bw, _ = wg_ref.shape

    @pl.when(pl.program_id(1) == 0)
    def _():
        xbuf[0:SUBLANES, :] = jnp.zeros((SUBLANES, c), F32)
        hcar[...] = jnp.zeros_like(hcar)

    xbuf[SUBLANES:, :] = xr_ref[...].astype(F32)
    lam = lam_ref[...]
    sp = jnp.maximum(-lam, 0.0) + jnp.log(1.0 + jnp.exp(-jnp.abs(lam)))
    rowmod = lax.broadcasted_iota(jnp.int32, (t // SUBLANES, SUBLANES, bw), 1)

    for n in range(nblk):
        sl = slice(n * bw, (n + 1) * bw)
        xc = cb_ref[:, sl] + cw_ref[3:4, sl] * xbuf[SUBLANES:SUBLANES + t, sl]
        for s in (1, 2, 3):
            xc = xc + cw_ref[3 - s:4 - s, sl] * xbuf[SUBLANES - s:SUBLANES - s + t, sl]
        g = jnp.dot(xc.astype(BF16), wg_ref[n], preferred_element_type=F32)
        r = _sigmoid(g[:, :bw] + ba_ref[:, sl])
        i = _sigmoid(g[:, bw:] + bx_ref[:, sl])
        log_a = -LRU_C * r * sp[:, sl]
        a = jnp.exp(log_a)
        b = xc * i * jnp.sqrt(-jnp.tanh(log_a) * (a * a + 1.0))
        a = a.reshape(t // SUBLANES, SUBLANES, bw)
        b = b.reshape(t // SUBLANES, SUBLANES, bw)
        for d in (1, 2, 4):
            keep = rowmod >= d
            a_sh = pltpu.roll(a, d, 1)
            b_sh = pltpu.roll(b, d, 1)
            b = jnp.where(keep, a * b_sh + b, b)
            a = jnp.where(keep, a * a_sh, a)
        a_s[:, sl] = a.reshape(t, bw)
        b_s[:, sl] = b.reshape(t, bw)

    xbuf[0:SUBLANES, :] = xbuf[t:t + SUBLANES, :]

    def group(gi, hb):
        rows = pl.ds(pl.multiple_of(gi * SUBLANES, SUBLANES), SUBLANES)
        h8 = a_s[rows, :] * hb + b_s[rows, :]
        h_s[rows, :] = h8
        return jnp.broadcast_to(h8[SUBLANES - 1:SUBLANES, :], (SUBLANES, c))

    hcar[...] = lax.fori_loop(0, t // SUBLANES, group, hcar[...])

    y = yr_ref[...].astype(F32)
    gelu = 0.5 * y * (1.0 + jnp.tanh(math.sqrt(2.0 / math.pi) * (y + 0.044715 * (y * y * y))))
    hg = (h_s[...] * gelu).astype(BF16)
    br = jnp.dot(hg, wo_ref[...], preferred_element_type=F32)
    out_ref[...] = (_sigmoid(ga_ref[...].astype(F32)) * br).astype(out_ref.dtype)


def _rglru(xr, yr, ga, cw, cb, wg, ba, bx, lam, wo):
    b, s, c = xr.shape
    t = T_RGLRU
    tile = pl.BlockSpec((None, t, c), lambda bi, j: (bi, j, 0))
    consts = (cw, cb, wg, ba, bx, lam, wo)
    return pl.pallas_call(
        _rglru_body,
        grid=(b, s // t),
        in_specs=[tile, tile, tile] + [_const_spec(a.shape) for a in consts],
        out_specs=tile,
        out_shape=jax.ShapeDtypeStruct((b, s, c), BF16),
        scratch_shapes=[
            pltpu.VMEM((t + SUBLANES, c), F32),
            pltpu.VMEM((t, c), F32),
            pltpu.VMEM((t, c), F32),
            pltpu.VMEM((t, c), F32),
            pltpu.VMEM((SUBLANES, c), F32),
        ],
        compiler_params=_params("parallel", "arbitrary"),
        name="rglru",
    )(xr, yr, ga, *consts)


def _mlaproj_body(cq_ref, ckv_ref, kpe_ref, posc_ref, posr_ref, qg_ref, kvg_ref, wqt_ref, wqrt_ref, wk_ref, wvt_ref,
                  gqn_ref, gqr_ref, gqrr_ref, gkn_ref, gkr_ref, gkrr_ref, frow_ref, fcol_ref,
                  qt_ref, k_ref, vt_ref):
    nh = wqt_ref.shape[0]
    t = cq_ref.shape[0]

    def latent_norm(ref, g_ref):
        z = ref[...]
        return z * lax.rsqrt(jnp.mean(z * z, axis=-1, keepdims=True) + EPS) * g_ref[...]

    cqn_t = latent_norm(cq_ref, qg_ref).T.astype(BF16)
    ckvn = latent_norm(ckv_ref, kvg_ref)
    ckvn_t = ckvn.T.astype(BF16)
    ckvn = ckvn.astype(BF16)

    ang_t = fcol_ref[...] * posr_ref[...]
    gr_cos = gqr_ref[...] * jnp.cos(ang_t)
    grr_sin = gqrr_ref[...] * jnp.sin(ang_t)
    qscale = QK_DIM ** -0.5 * math.log2(math.e)
    for h in range(nh):
        qh = jnp.dot(wqt_ref[h], cqn_t, preferred_element_type=F32)
        qr = jnp.dot(wqrt_ref[h], cqn_t, preferred_element_type=F32)
        rstd = lax.rsqrt(jnp.mean(qh * qh, axis=0, keepdims=True) + EPS) * qscale
        qt_ref[h, :QK_NOPE, :] = (qh[:QK_NOPE] * gqn_ref[...] * rstd).astype(qt_ref.dtype)
        qt_ref[h, QK_NOPE:, :] = ((qh[QK_NOPE:] * gr_cos + qr * grr_sin) * rstd).astype(qt_ref.dtype)

    ang = posc_ref[...] * frow_ref[...]
    kpe = kpe_ref[:, :QK_ROPE]
    kpr = kpe_ref[:, QK_ROPE:]
    k_rot = kpe * gkr_ref[...] * jnp.cos(ang) + kpr * gkrr_ref[...] * jnp.sin(ang)
    ss_pe = jnp.sum(kpe * kpe, axis=-1, keepdims=True)
    ones_rows = (lax.broadcasted_iota(jnp.int32, (V_ROWS - V_DIM, t), 0) == 0).astype(vt_ref.dtype)
    for h in range(nh):
        kn = jnp.dot(ckvn, wk_ref[h], preferred_element_type=F32)
        rstd = lax.rsqrt((jnp.sum(kn * kn, axis=-1, keepdims=True) + ss_pe) * (1.0 / QK_DIM) + EPS)
        k_ref[h, :, :QK_NOPE] = (kn * gkn_ref[...] * rstd).astype(k_ref.dtype)
        k_ref[h, :, QK_NOPE:] = (k_rot * rstd).astype(k_ref.dtype)
        vt_ref[h, :V_DIM, :] = jnp.dot(wvt_ref[h], ckvn_t, preferred_element_type=F32).astype(vt_ref.dtype)
        vt_ref[h, V_DIM:, :] = ones_rows


def _mla_proj(cq, ckv, kpe2, pos_col, pos_row, consts):
    b, s, _ = cq.shape
    t = T_MLAPROJ
    nh = N_HEADS

    def tile(c):
        return pl.BlockSpec((None, t, c), lambda bi, j: (bi, j, 0))

    return pl.pallas_call(
        _mlaproj_body,
        grid=(b, s // t),
        in_specs=[tile(cq.shape[-1]), tile(ckv.shape[-1]), tile(kpe2.shape[-1]), tile(1),
                  pl.BlockSpec((None, 1, t), lambda bi, j: (bi, 0, j))]
        + [_const_spec(a.shape) for a in consts],
        out_specs=[pl.BlockSpec((None, nh, QK_DIM, t), lambda bi, j: (bi, 0, 0, j)),
                   pl.BlockSpec((None, nh, t, QK_DIM), lambda bi, j: (bi, 0, j, 0)),
                   pl.BlockSpec((None, nh, None, V_ROWS, t), lambda bi, j: (bi, 0, j, 0, 0))],
        out_shape=[jax.ShapeDtypeStruct((b, nh, QK_DIM, s), BF16),
                   jax.ShapeDtypeStruct((b, nh, s, QK_DIM), BF16),
                   jax.ShapeDtypeStruct((b, nh, s // t, V_ROWS, t), BF16)],
        compiler_params=_params("parallel", "parallel"),
        name="mla_proj",
    )(cq, ckv, kpe2, pos_col, pos_row, *consts)


def _attn_body(qt_ref, k_ref, vt_ref, o_ref, m_s, acc_s):
    tq = qt_ref.shape[1]
    tk = vt_ref.shape[-1]
    dv = o_ref.shape[-1]
    nchain = tq // Q_CHAIN
    per_tile = tq // tk
    nfull = pl.program_id(2) * per_tile
    m_s[...] = jnp.full(m_s.shape, -jnp.inf, F32)
    acc_s[...] = jnp.zeros(acc_s.shape, F32)

    def scores(j, c, key_minus_query):
        kb = k_ref[pl.ds(pl.multiple_of(j * tk, tk), tk), :]
        s = jnp.dot(kb, qt_ref[:, c * Q_CHAIN:(c + 1) * Q_CHAIN], preferred_element_type=F32)
        if key_minus_query is not None:
            krow = lax.broadcasted_iota(jnp.int32, s.shape, 0) + key_minus_query
            qcol = lax.broadcasted_iota(jnp.int32, s.shape, 1)
            s = jnp.where(krow <= qcol, s, -jnp.inf)
        return s

    def softmax_pv(j, c, s):
        cs = slice(c * Q_CHAIN, (c + 1) * Q_CHAIN)
        r = s
        while r.shape[0] > SUBLANES:
            half = r.shape[0] // 2
            r = jnp.maximum(r[:half], r[half:])
        m_old = m_s[:, cs]
        m_new = jnp.maximum(m_old, jnp.max(r, axis=0, keepdims=True))
        alpha = jnp.exp2(m_old - m_new)
        p = jnp.exp2(s - m_new).astype(BF16)
        m_s[:, cs] = m_new
        acc_s[:, cs] = alpha * acc_s[:, cs] + jnp.dot(vt_ref[j], p, preferred_element_type=F32)

    def run(todo):
        pending = {i: scores(*todo[i]) for i in range(min(QK_AHEAD, len(todo)))}
        for i, (j, c, _) in enumerate(todo):
            if i + QK_AHEAD < len(todo):
                pending[i + QK_AHEAD] = scores(*todo[i + QK_AHEAD])
            softmax_pv(j, c, pending.pop(i))

    def full_blocks(jj, carry):
        run([(jj * per_tile + u, c, None) for u in range(per_tile) for c in range(nchain)])
        return carry

    lax.fori_loop(0, nfull // per_tile, full_blocks, 0)

    diagonal = []
    for u in range(per_tile):
        for c in range(nchain):
            k0, q0 = u * tk, c * Q_CHAIN
            if k0 > q0 + Q_CHAIN - 1:
                continue
            diagonal.append((nfull + u, c, None if k0 + tk - 1 <= q0 else k0 - q0))
    run(diagonal)

    out_t = acc_s[:dv, :] / acc_s[dv:dv + 1, :]
    o_ref[...] = out_t.T.astype(o_ref.dtype)


def _attention(qt, k, vt):
    b, nh, dqk, s = qt.shape
    nkb, vrows, tk = vt.shape[2:]
    tq = T_Q
    return pl.pallas_call(
        _attn_body,
        grid=(b, nh, s // tq),
        in_specs=[pl.BlockSpec((None, None, dqk, tq), lambda bi, h, i: (bi, h, 0, i)),
                  pl.BlockSpec((None, None, s, dqk), lambda bi, h, i: (bi, h, 0, 0)),
                  pl.BlockSpec((None, None, nkb, vrows, tk), lambda bi, h, i: (bi, h, 0, 0, 0))],
        out_specs=pl.BlockSpec((None, tq, V_DIM), lambda bi, h, i: (bi, i, h)),
        out_shape=jax.ShapeDtypeStruct((b, s, nh * V_DIM), BF16),
        scratch_shapes=[pltpu.VMEM((1, tq), F32), pltpu.VMEM((vrows, tq), F32)],
        compiler_params=_params("parallel", "parallel", "arbitrary"),
        name="attention",
    )(qt, k, vt)


def _merge_body(o_ref, ga_ref, gb_ref, x_ref, wmo_ref, wout_ref, g2_ref, wrh_ref, wrl_ref, rb_ref,
                hrow_ref, rec_ref, cnt_ref, cnt_s, *, n_groups, per_group):
    @pl.when(pl.program_id(0) == 0)
    def _():
        cnt_s[...] = jnp.zeros_like(cnt_s)

    br_b = jnp.dot(o_ref[...], wmo_ref[...], preferred_element_type=F32)
    merged = ga_ref[...].astype(F32) + _sigmoid(gb_ref[...].astype(F32)) * br_b
    h = x_ref[...] + jnp.dot(merged.astype(BF16), wout_ref[...], preferred_element_type=F32)

    u = h * lax.rsqrt(jnp.mean(h * h, axis=-1, keepdims=True) + EPS) * g2_ref[...]
    u_hi = u.astype(BF16)
    u_lo = (u - u_hi.astype(F32)).astype(BF16)
    logits = (jnp.dot(u_hi, wrh_ref[...], preferred_element_type=F32)
              + jnp.dot(u_lo, wrh_ref[...], preferred_element_type=F32)
              + jnp.dot(u_hi, wrl_ref[...], preferred_element_type=F32)) + rb_ref[...]

    t, nl = logits.shape
    lane = lax.broadcasted_iota(jnp.int32, (t, nl), 1)
    neg = -jnp.inf

    def first_argmax(z, zmax):
        return jnp.min(jnp.where(z == zmax, lane, nl), axis=-1, keepdims=True)

    zg = jnp.where(lane < n_groups, logits, neg)
    mg = jnp.max(zg, axis=-1, keepdims=True)
    g_sel = first_argmax(zg, mg)
    g_prob = 1.0 / jnp.sum(jnp.exp(zg - mg), axis=-1, keepdims=True)
    lo = n_groups + g_sel * per_group
    ze = jnp.where((lane >= lo) & (lane < lo + per_group), logits, neg)
    m1 = jnp.max(ze, axis=-1, keepdims=True)
    i1 = first_argmax(ze, m1)
    ze2 = jnp.where(lane == i1, neg, ze)
    m2 = jnp.max(ze2, axis=-1, keepdims=True)
    i2 = first_argmax(ze2, m2)
    e21 = jnp.exp(m2 - m1)
    w1 = g_prob / (1.0 + e21)
    w2 = w1 * e21

    cnt_s[...] = cnt_s[...] + jnp.sum(jnp.where(lane == g_sel, 1.0, 0.0), axis=0, keepdims=True)
    cnt_ref[...] = cnt_s[...]

    rec = jnp.where(lane == 0, (i1 - lo).astype(F32), 0.0)
    for col, val in ((1, (i2 - lo).astype(F32)), (2, w1), (3, w2), (4, g_sel.astype(F32))):
        rec = jnp.where(lane == col, val, rec)
    rec_ref[...] = rec

    _natural_to_rows(hrow_ref, h, TOKEN_SLAB)
    hrow_ref[pl.ds(ROW_CHUNKS, t, stride=TOKEN_SLAB), :] = rec
    for c in range(ROW_CHUNKS + 1, TOKEN_SLAB):
        hrow_ref[pl.ds(c, t, stride=TOKEN_SLAB), :] = jnp.zeros((t, nl), F32)


def _merge(o2, ga2, gb2, x2, wmo, wout, g2, wrh, wrl, rb, n_groups, per_group):
    n, d = x2.shape
    t = T_MERGE
    consts = (wmo, wout, g2, wrh, wrl, rb)
    tile = pl.BlockSpec((t, d), lambda i: (i, 0))
    return pl.pallas_call(
        functools.partial(_merge_body, n_groups=n_groups, per_group=per_group),
        grid=(n // t,),
        in_specs=[tile, tile, tile, tile] + [_const_spec(a.shape) for a in consts],
        out_specs=[pl.BlockSpec((t * TOKEN_SLAB, LANES), lambda i: (i, 0)),
                   pl.BlockSpec((t, ROUTER_LANES), lambda i: (i, 0)),
                   pl.BlockSpec((SUBLANES, ROUTER_LANES), lambda i: (0, 0))],
        out_shape=[jax.ShapeDtypeStruct((n * TOKEN_SLAB, LANES), F32),
                   jax.ShapeDtypeStruct((n, ROUTER_LANES), F32),
                   jax.ShapeDtypeStruct((SUBLANES, ROUTER_LANES), F32)],
        scratch_shapes=[pltpu.VMEM((SUBLANES, ROUTER_LANES), F32)],
        compiler_params=_params("arbitrary"),
        name="merge_router",
    )(o2, ga2, gb2, x2, *consts)


def _slab(ref, row, slab=ROW_CHUNKS):
    return ref.at[pl.ds(pl.multiple_of(row * slab, slab), slab), :]


def _experts_body(order_ref, first_ref, nvalid_ref, bg_ref, nb_ref, h_hbm, g2_ref, w1_ref, w3_ref, w2_ref, y_hbm,
                  xbuf0, xbuf1, xbuf2, ybuf0, ybuf1, ybuf2, gsem, ssem):
    blk = pl.program_id(0)
    rows = MOE_BLOCK
    n_used = nb_ref[0]
    n_tok = order_ref.shape[0]
    per_group = w1_ref.shape[0]
    xbufs = (xbuf0, xbuf1, xbuf2)
    ybufs = (ybuf0, ybuf1, ybuf2)
    nbuf = len(xbufs)

    def token(b, r):
        return order_ref[jnp.minimum(first_ref[b] + r, n_tok - 1)], r < nvalid_ref[b]

    def start_gather(b, s):
        for r in range(rows):
            tok, valid = token(b, r)
            tok = jnp.where(valid, tok, 0)
            pltpu.make_async_copy(_slab(h_hbm, tok, TOKEN_SLAB), _slab(xbufs[s], r, TOKEN_SLAB),
                                  gsem.at[s]).start(priority=r % 2)

    def wait_gather(s):
        pltpu.make_async_copy(h_hbm.at[pl.ds(0, rows * TOKEN_SLAB), :], xbufs[s], gsem.at[s]).wait()

    def start_scatter(b, s):
        for r in range(rows):
            tok, valid = token(b, r)
            dst = jnp.where(valid, tok, n_tok + r)
            pltpu.make_async_copy(_slab(ybufs[s], r), _slab(y_hbm, dst), ssem.at[s]).start(priority=r % 2)

    def wait_scatter(s):
        pltpu.make_async_copy(ybufs[s], y_hbm.at[pl.ds(0, rows * ROW_CHUNKS), :], ssem.at[s]).wait()

    @pl.when(blk == 0)
    def _():
        ybuf1[...] = jnp.zeros_like(ybuf1)
        spare = y_hbm.at[pl.ds(y_hbm.shape[0] - rows * ROW_CHUNKS, rows * ROW_CHUNKS), :]
        fill = pltpu.make_async_copy(ybuf1, spare, ssem.at[1])
        fill.start()
        fill.wait()

    for first in range(nbuf - 1):
        @pl.when((blk == 0) & (first < n_used))
        def _():
            start_gather(first, first)

    def step(s, has_ahead):
        wait_gather(s)
        if has_ahead:
            start_gather(blk + nbuf - 1, (s + nbuf - 1) % nbuf)
        x = _rows_to_natural(xbufs[s], rows, TOKEN_SLAB)
        rec = xbufs[s][pl.ds(ROW_CHUNKS, rows, stride=TOKEN_SLAB), :]
        pick = [rec[:, kk:kk + 1] for kk in range(TOP_K)]
        gate = [rec[:, TOP_K + kk:TOP_K + kk + 1] for kk in range(TOP_K)]
        u = (x * lax.rsqrt(jnp.mean(x * x, axis=-1, keepdims=True) + EPS) * g2_ref[...]).astype(BF16)
        hidden = []
        for j in range(per_group):
            a = jnp.dot(u, w1_ref[j], preferred_element_type=F32)
            c = jnp.dot(u, w3_ref[j], preferred_element_type=F32)
            g = jnp.zeros_like(gate[0])
            for kk in range(TOP_K):
                g = jnp.where(pick[kk] == float(j), gate[kk], g)
            routed = functools.reduce(jnp.logical_or, [pick[kk] == float(j) for kk in range(TOP_K)])
            hidden.append(jnp.where(routed, a * _sigmoid(a) * c * g, 0.0).astype(BF16))
        w2_all = w2_ref[...].reshape(per_group * w2_ref.shape[1], w2_ref.shape[2])
        y = x + jnp.dot(jnp.concatenate(hidden, axis=1), w2_all, preferred_element_type=F32)
        _natural_to_rows(ybufs[s], y)

        @pl.when(blk > 0)
        def _():
            wait_scatter((s + nbuf - 1) % nbuf)
        start_scatter(blk, s)

    slot = lax.rem(blk, nbuf)
    for s in range(nbuf):
        @pl.when((slot == s) & (blk + nbuf - 1 < n_used))
        def _():
            step(s, True)

        @pl.when((slot == s) & (blk + nbuf - 1 >= n_used) & (blk < n_used))
        def _():
            step(s, False)

            @pl.when(blk + 1 == n_used)
            def _():
                wait_scatter(s)


def _experts(order, block_first, block_nvalid, block_g, n_used, h_slabs, g2, w1, w3, w2):
    d = g2.shape[-1]
    rows = MOE_BLOCK
    nblk = block_g.shape[0]
    per_group, de = w1.shape[1], w1.shape[-1]
    n_out_rows = order.shape[0] + rows
    grid_spec = pltpu.PrefetchScalarGridSpec(
        num_scalar_prefetch=5,
        grid=(nblk,),
        in_specs=[
            pl.BlockSpec(memory_space=pl.ANY),
            pl.BlockSpec(g2.shape, lambda i, *_: (0, 0)),
            pl.BlockSpec((None, per_group, d, de), lambda i, order, first, nvalid, bg, nb: (bg[i], 0, 0, 0)),
            pl.BlockSpec((None, per_group, d, de), lambda i, order, first, nvalid, bg, nb: (bg[i], 0, 0, 0)),
            pl.BlockSpec((None, per_group, de, d), lambda i, order, first, nvalid, bg, nb: (bg[i], 0, 0, 0)),
        ],
        out_specs=pl.BlockSpec(memory_space=pl.ANY),
        scratch_shapes=[pltpu.VMEM((rows * TOKEN_SLAB, LANES), F32)] * 3
        + [pltpu.VMEM((rows * ROW_CHUNKS, LANES), F32)] * 3
        + [pltpu.SemaphoreType.DMA((3,)), pltpu.SemaphoreType.DMA((3,))],
    )
    return pl.pallas_call(
        _experts_body,
        grid_spec=grid_spec,
        out_shape=jax.ShapeDtypeStruct((n_out_rows * ROW_CHUNKS, LANES), F32),
        compiler_params=_params("arbitrary"),
        name="experts",
    )(order, block_first, block_nvalid, block_g, n_used, h_slabs, g2, w1, w3, w2)


def _unrow_body(rows_ref, out_ref):
    out_ref[...] = _rows_to_natural(rows_ref, out_ref.shape[0])


def _unrow(out_rows, n):
    d = ROW_CHUNKS * LANES
    t = T_COMBINE
    return pl.pallas_call(
        _unrow_body,
        grid=(n // t,),
        in_specs=[pl.BlockSpec((t * ROW_CHUNKS, LANES), lambda i: (i, 0))],
        out_specs=pl.BlockSpec((t, d), lambda i: (i, 0)),
        out_shape=jax.ShapeDtypeStruct((n, d), F32),
        compiler_params=_params("parallel"),
        name="unrow",
    )(out_rows)


def _dispatch_tables(flat_e, counts):
    n_experts = counts.shape[0]
    m = flat_e.shape[0]
    m_pad = m + n_experts * MOE_BLOCK
    n_blocks = m_pad // MOE_BLOCK
    padded = ((counts + MOE_BLOCK - 1) // MOE_BLOCK) * MOE_BLOCK
    expert = jnp.arange(n_experts, dtype=jnp.int32)
    upto = expert[None, :] <= expert[:, None]
    start = jnp.sum(jnp.where(upto, counts[None, :], 0), axis=1) - counts
    pad_end = jnp.sum(jnp.where(upto, padded[None, :], 0), axis=1)
    pad_start = pad_end - padded
    block_row = jnp.arange(n_blocks, dtype=jnp.int32) * MOE_BLOCK
    block_e = jnp.minimum(jnp.sum(pad_end[None, :] <= block_row[:, None], axis=1), n_experts - 1).astype(jnp.int32)
    n_used = (jnp.sum(padded) // MOE_BLOCK).astype(jnp.int32).reshape(1)
    mine = block_e[:, None] == expert[None, :]

    def of_block(table):
        return jnp.sum(jnp.where(mine, table[None, :], 0), axis=1)

    order = jnp.argsort(flat_e, stable=True).astype(jnp.int32)
    off = block_row - of_block(pad_start)
    block_first = (of_block(start) + off).astype(jnp.int32)
    block_nvalid = jnp.clip(of_block(counts) - off, 0, MOE_BLOCK).astype(jnp.int32)
    return order, block_first, block_nvalid, block_e, n_used


def _layer(x, positions, norm1_g, w_in, conv_w, conv_b, lru_wa, lru_ba, lru_wx, lru_bx, lru_lambda, w_rnn_o,
           q_norm_g, w_uq, kv_norm_g, w_ukv, qk_norm_q_g, qk_norm_k_g, w_mla_o, w_out, norm2_g,
           router_wg, router_bg, router_we, router_be, exp_w1, exp_w3, exp_w2):
    b, s, d = x.shape
    n = b * s
    d_rnn = conv_w.shape[-1]
    q_lora = q_norm_g.shape[-1]
    kv_lora = kv_norm_g.shape[-1]
    nh, half = N_HEADS, QK_ROPE // 2
    row = lambda a: a.reshape(1, -1).astype(F32)
    col = lambda a: a.reshape(-1, 1).astype(F32)

    o_xr, o_yr, o_cq, o_ckv, o_kpe, o_ga, o_gb = np.cumsum([0, d_rnn, d_rnn, q_lora, kv_lora, QK_ROPE, d])
    w_kpe = w_in[:, o_kpe:o_kpe + QK_ROPE]
    w_kpe_rot = jnp.concatenate([-w_kpe[:, half:], w_kpe[:, :half]], axis=1)
    w_in_k = jnp.concatenate([w_in[:, :o_kpe], w_in[:, o_ga:], w_kpe, w_kpe_rot], axis=1).astype(BF16)
    widths = (d_rnn, d_rnn, q_lora, kv_lora, d, d, 2 * QK_ROPE)
    dtypes = (BF16, BF16, F32, F32, BF16, BF16, F32)

    x2 = x.reshape(n, d)
    xr, yr, cq, ckv, ga, gb, kpe2 = _in_proj(x2, row(norm1_g), w_in_k, widths, dtypes)

    w_gate = jnp.concatenate([lru_wa, lru_wx], axis=-1).astype(BF16)
    gated_a = _rglru(xr.reshape(b, s, d_rnn), yr.reshape(b, s, d_rnn), ga.reshape(b, s, d),
                     conv_w.astype(F32), row(conv_b), w_gate, row(lru_ba), row(lru_bx), row(lru_lambda),
                     w_rnn_o.astype(BF16))

    def rot(v):
        return jnp.concatenate([-v[..., half:], v[..., :half]], axis=-1)

    def swap(v):
        return jnp.concatenate([v[..., half:], v[..., :half]], axis=-1)

    wq = w_uq.reshape(q_lora, nh, QK_DIM).transpose(1, 0, 2)
    wq_t = wq.transpose(0, 2, 1).astype(BF16)
    wqr_t = rot(wq[..., QK_NOPE:]).transpose(0, 2, 1).astype(BF16)
    wkv = w_ukv.reshape(kv_lora, nh, QK_NOPE + V_DIM).transpose(1, 0, 2)
    wk = wkv[..., :QK_NOPE].astype(BF16)
    wv_t = wkv[..., QK_NOPE:].transpose(0, 2, 1).astype(BF16)
    gq = qk_norm_q_g.astype(F32)
    gk = qk_norm_k_g.astype(F32)
    inv_freq = ROPE_THETA ** (-jnp.arange(0, QK_ROPE, 2, dtype=F32) / QK_ROPE)
    freq = jnp.concatenate([inv_freq, inv_freq])
    consts = (row(q_norm_g), row(kv_norm_g), wq_t, wqr_t, wk, wv_t,
              col(gq[:QK_NOPE]), col(gq[QK_NOPE:]), col(swap(gq[QK_NOPE:])),
              row(gk[:QK_NOPE]), row(gk[QK_NOPE:]), row(swap(gk[QK_NOPE:])), row(freq), col(freq))
    pos = positions.astype(F32)
    qt, k, vt = _mla_proj(cq.reshape(b, s, q_lora), ckv.reshape(b, s, kv_lora), kpe2.reshape(b, s, 2 * QK_ROPE),
                          pos.reshape(b, s, 1), pos.reshape(b, 1, s), consts)
    o = _attention(qt, k, vt)

    n_groups = router_wg.shape[-1]
    n_experts = router_we.shape[-1]
    per_group = n_experts // n_groups
    wr = jnp.concatenate([router_wg, router_we,
                          jnp.zeros((d, ROUTER_LANES - n_groups - n_experts), F32)], axis=1).astype(F32)
    wr_hi = wr.astype(BF16)
    wr_lo = (wr - wr_hi.astype(F32)).astype(BF16)
    rb = jnp.concatenate([router_bg, router_be,
                          jnp.zeros((ROUTER_LANES - n_groups - n_experts,), F32)]).reshape(1, ROUTER_LANES).astype(F32)
    h_slabs, rec, cnt = _merge(o.reshape(n, d), gated_a.reshape(n, d), gb, x2, w_mla_o.astype(BF16),
                               w_out.astype(BF16), row(norm2_g), wr_hi, wr_lo, rb, n_groups, per_group)

    group_id = rec[:, 2 * TOP_K].astype(jnp.int32)
    counts = cnt[0, :n_groups].astype(jnp.int32)
    order, block_first, block_nvalid, block_g, n_used = _dispatch_tables(group_id, counts)
    de = exp_w1.shape[-1]
    out_rows = _experts(order, block_first, block_nvalid, block_g, n_used, h_slabs, row(norm2_g),
                        exp_w1.astype(BF16).reshape(n_groups, per_group, d, de),
                        exp_w3.astype(BF16).reshape(n_groups, per_group, d, de),
                        exp_w2.astype(BF16).reshape(n_groups, per_group, de, d))
    return _unrow(out_rows, n).reshape(b, s, d)


def kernel(x, positions, norm1_g, w_in, conv_w, conv_b, lru_wa, lru_ba, lru_wx, lru_bx, lru_lambda, w_rnn_o, q_norm_g, w_uq, kv_norm_g, w_ukv, qk_norm_q_g, qk_norm_k_g, w_mla_o, w_out, norm2_g, router_wg, router_bg, router_we, router_be, exp_w1, exp_w3, exp_w2):
    args = (norm1_g, w_in, conv_w, conv_b, lru_wa, lru_ba, lru_wx, lru_bx, lru_lambda, w_rnn_o, q_norm_g, w_uq,
            kv_norm_g, w_ukv, qk_norm_q_g, qk_norm_k_g, w_mla_o, w_out, norm2_g, router_wg, router_bg, router_we,
            router_be, exp_w1, exp_w3, exp_w2)
    depth = norm1_g.shape[0]
    h = x
    for l in range(depth):
        h = _layer(h, positions, *[a[l] for a in args])
    return h
```

```python
import functools
import math

import numpy as np
import jax
import jax.numpy as jnp
from jax import lax
from jax.experimental import pallas as pl
from jax.experimental.pallas import tpu as pltpu

F32 = jnp.float32
BF16 = jnp.bfloat16
EPS = 1e-6

N_HEADS = 8
QK_NOPE = 128
QK_ROPE = 64
QK_DIM = QK_NOPE + QK_ROPE
V_DIM = 128
ROPE_THETA = 10000.0
LRU_C = 8.0
TOP_K = 2
MOE_BLOCK = 256
ROUTER_LANES = 128

SUBLANES = 8
LANES = 128
VMEM_LIMIT_BYTES = 56 * 1024 * 1024

T_INPROJ = 256
T_RGLRU = 256
T_KV = 512
T_MLAPROJ = T_KV
T_Q = 2048
Q_CHAIN = 512
QK_AHEAD = 3
V_ROWS = V_DIM + 16
T_MERGE = 256
T_COMBINE = 512
ROW_CHUNKS = 8
TOKEN_SLAB = 16


def _params(*sem):
    return pltpu.CompilerParams(dimension_semantics=sem, vmem_limit_bytes=VMEM_LIMIT_BYTES)


def _sigmoid(z):
    return 0.5 * jnp.tanh(0.5 * z) + 0.5


def _const_spec(shape):
    n = len(shape)
    return pl.BlockSpec(shape, lambda *_: (0,) * n)


def _rows_to_natural(ref, rows, slab=ROW_CHUNKS):
    return jnp.concatenate([ref[pl.ds(c, rows, stride=slab), :] for c in range(ROW_CHUNKS)], axis=1)


def _natural_to_rows(ref, value, slab=ROW_CHUNKS):
    rows = value.shape[0]
    for c in range(ROW_CHUNKS):
        ref[pl.ds(c, rows, stride=slab), :] = value[:, c * LANES:(c + 1) * LANES]


def _inproj_body(x_ref, g_ref, w_ref, *out_refs):
    x = x_ref[...]
    u = x * lax.rsqrt(jnp.mean(x * x, axis=-1, keepdims=True) + EPS) * g_ref[...]
    u = u.astype(BF16)
    off = 0
    for ref in out_refs:
        n = ref.shape[-1]
        ref[...] = jnp.dot(u, w_ref[:, off:off + n], preferred_element_type=F32).astype(ref.dtype)
        off += n


def _in_proj(x2, g, w, widths, dtypes):
    n, d = x2.shape
    t = T_INPROJ
    return pl.pallas_call(
        _inproj_body,
        grid=(n // t,),
        in_specs=[pl.BlockSpec((t, d), lambda i: (i, 0)), _const_spec(g.shape), _const_spec(w.shape)],
        out_specs=[pl.BlockSpec((t, c), lambda i: (i, 0)) for c in widths],
        out_shape=[jax.ShapeDtypeStruct((n, c), dt) for c, dt in zip(widths, dtypes)],
        compiler_params=_params("parallel"),
        name="in_proj",
    )(x2, g, w)


def _rglru_body(xr_ref, yr_ref, ga_ref, cw_ref, cb_ref, wg_ref, ba_ref, bx_ref, lam_ref, wo_ref,
                out_ref, xbuf, a_s, b_s, h_s, hcar):
    t, c = xr_ref.shape
    nblk, bw, _ = wg_ref.shape

    @pl.when(pl.program_id(1) == 0)
    def _():
        xbuf[0:SUBLANES, :] = jnp.zeros((SUBLANES, c), F32)
        hcar[...] = jnp.zeros_like(hcar)

    xbuf[SUBLANES:, :] = xr_ref[...].astype(F32)
    lam = lam_ref[...]
    sp = jnp.maximum(-lam, 0.0) + jnp.log(1.0 + jnp.exp(-jnp.abs(lam)))
    rowmod = lax.broadcasted_iota(jnp.int32, (t // SUBLANES, SUBLANES, bw), 1)

    for n in range(nblk):
        sl = slice(n * bw, (n + 1) * bw)
        xc = cb_ref[:, sl] + cw_ref[3:4, sl] * xbuf[SUBLANES:SUBLANES + t, sl]
        for s in (1, 2, 3):
            xc = xc + cw_ref[3 - s:4 - s, sl] * xbuf[SUBLANES - s:SUBLANES - s + t, sl]
        g = jnp.dot(xc.astype(BF16), wg_ref[n], preferred_element_type=F32)
        r = _sigmoid(g[:, :bw] + ba_ref[:, sl])
        i = _sigmoid(g[:, bw:] + bx_ref[:, sl])
        log_a = -LRU_C * r * sp[:, sl]
        a = jnp.exp(log_a)
        b = xc * i * jnp.sqrt(-jnp.tanh(log_a) * (a * a + 1.0))
        a = a.reshape(t // SUBLANES, SUBLANES, bw)
        b = b.reshape(t // SUBLANES, SUBLANES, bw)
        for d in (1, 2, 4):
            keep = rowmod >= d
            a_sh = pltpu.roll(a, d, 1)
            b_sh = pltpu.roll(b, d, 1)
            b = jnp.where(keep, a * b_sh + b, b)
            a = jnp.where(keep, a * a_sh, a)
        a_s[:, sl] = a.reshape(t, bw)
        b_s[:, sl] = b.reshape(t, bw)

    xbuf[0:SUBLANES, :] = xbuf[t:t + SUBLANES, :]

    def group(gi, hb):
        rows = pl.ds(pl.multiple_of(gi * SUBLANES, SUBLANES), SUBLANES)
        h8 = a_s[rows, :] * hb + b_s[rows, :]
        h_s[rows, :] = h8
        return jnp.broadcast_to(h8[SUBLANES - 1:SUBLANES, :], (SUBLANES, c))

    hcar[...] = lax.fori_loop(0, t // SUBLANES, group, hcar[...])

    y = yr_ref[...].astype(F32)
    gelu = 0.5 * y * (1.0 + jnp.tanh(math.sqrt(2.0 / math.pi) * (y + 0.044715 * (y * y * y))))
    hg = (h_s[...] * gelu).astype(BF16)
    br = jnp.dot(hg, wo_ref[...], preferred_element_type=F32)
    out_ref[...] = (_sigmoid(ga_ref[...].astype(F32)) * br).astype(out_ref.dtype)


def _rglru(xr, yr, ga, cw, cb, wg, ba, bx, lam, wo):
    b, s, c = xr.shape
    t = T_RGLRU
    tile = pl.BlockSpec((None, t, c), lambda bi, j: (bi, j, 0))
    consts = (cw, cb, wg, ba, bx, lam, wo)
    return pl.pallas_call(
        _rglru_body,
        grid=(b, s // t),
        in_specs=[tile, tile, tile] + [_const_spec(a.shape) for a in consts],
        out_specs=tile,
        out_shape=jax.ShapeDtypeStruct((b, s, c), BF16),
        scratch_shapes=[
            pltpu.VMEM((t + SUBLANES, c), F32),
            pltpu.VMEM((t, c), F32),
            pltpu.VMEM((t, c), F32),
            pltpu.VMEM((t, c), F32),
            pltpu.VMEM((SUBLANES, c), F32),
        ],
        compiler_params=_params("parallel", "arbitrary"),
        name="rglru",
    )(xr, yr, ga, *consts)


def _mlaproj_body(cq_ref, ckv_ref, kpe_ref, posc_ref, posr_ref, qg_ref, kvg_ref, wqt_ref, wqrt_ref, wk_ref, wvt_ref,
                  gqn_ref, gqr_ref, gqrr_ref, gkn_ref, gkr_ref, gkrr_ref, frow_ref, fcol_ref,
                  qt_ref, k_ref, vt_ref):
    nh = wqt_ref.shape[0]
    t = cq_ref.shape[0]

    def latent_norm(ref, g_ref):
        z = ref[...]
        return z * lax.rsqrt(jnp.mean(z * z, axis=-1, keepdims=True) + EPS) * g_ref[...]

    cqn_t = latent_norm(cq_ref, qg_ref).T.astype(BF16)
    ckvn = latent_norm(ckv_ref, kvg_ref)
    ckvn_t = ckvn.T.astype(BF16)
    ckvn = ckvn.astype(BF16)

    ang_t = fcol_ref[...] * posr_ref[...]
    gr_cos = gqr_ref[...] * jnp.cos(ang_t)
    grr_sin = gqrr_ref[...] * jnp.sin(ang_t)
    qscale = QK_DIM ** -0.5 * math.log2(math.e)
    for h in range(nh):
        qh = jnp.dot(wqt_ref[h], cqn_t, preferred_element_type=F32)
        qr = jnp.dot(wqrt_ref[h], cqn_t, preferred_element_type=F32)
        rstd = lax.rsqrt(jnp.mean(qh * qh, axis=0, keepdims=True) + EPS) * qscale
        qt_ref[h, :QK_NOPE, :] = (qh[:QK_NOPE] * gqn_ref[...] * rstd).astype(qt_ref.dtype)
        qt_ref[h, QK_NOPE:, :] = ((qh[QK_NOPE:] * gr_cos + qr * grr_sin) * rstd).astype(qt_ref.dtype)

    ang = posc_ref[...] * frow_ref[...]
    kpe = kpe_ref[:, :QK_ROPE]
    kpr = kpe_ref[:, QK_ROPE:]
    k_rot = kpe * gkr_ref[...] * jnp.cos(ang) + kpr * gkrr_ref[...] * jnp.sin(ang)
    ss_pe = jnp.sum(kpe * kpe, axis=-1, keepdims=True)
    ones_rows = (lax.broadcasted_iota(jnp.int32, (V_ROWS - V_DIM, t), 0) == 0).astype(vt_ref.dtype)
    for h in range(nh):
        kn = jnp.dot(ckvn, wk_ref[h], preferred_element_type=F32)
        rstd = lax.rsqrt((jnp.sum(kn * kn, axis=-1, keepdims=True) + ss_pe) * (1.0 / QK_DIM) + EPS)
        k_ref[h, :, :QK_NOPE] = (kn * gkn_ref[...] * rstd).astype(k_ref.dtype)
        k_ref[h, :, QK_NOPE:] = (k_rot * rstd).astype(k_ref.dtype)
        vt_ref[h, :V_DIM, :] = jnp.dot(wvt_ref[h], ckvn_t, preferred_element_type=F32).astype(vt_ref.dtype)
        vt_ref[h, V_DIM:, :] = ones_rows


def _mla_proj(cq, ckv, kpe2, pos_col, pos_row, consts):
    b, s, _ = cq.shape
    t = T_MLAPROJ
    nh = N_HEADS

    def tile(c):
        return pl.BlockSpec((None, t, c), lambda bi, j: (bi, j, 0))

    return pl.pallas_call(
        _mlaproj_body,
        grid=(b, s // t),
        in_specs=[tile(cq.shape[-1]), tile(ckv.shape[-1]), tile(kpe2.shape[-1]), tile(1),
                  pl.BlockSpec((None, 1, t), lambda bi, j: (bi, 0, j))]
        + [_const_spec(a.shape) for a in consts],
        out_specs=[pl.BlockSpec((None, nh, QK_DIM, t), lambda bi, j: (bi, 0, 0, j)),
                   pl.BlockSpec((None, nh, t, QK_DIM), lambda bi, j: (bi, 0, j, 0)),
                   pl.BlockSpec((None, nh, None, V_ROWS, t), lambda bi, j: (bi, 0, j, 0, 0))],
        out_shape=[jax.ShapeDtypeStruct((b, nh, QK_DIM, s), BF16),
                   jax.ShapeDtypeStruct((b, nh, s, QK_DIM), BF16),
                   jax.ShapeDtypeStruct((b, nh, s // t, V_ROWS, t), BF16)],
        compiler_params=_params("parallel", "parallel"),
        name="mla_proj",
    )(cq, ckv, kpe2, pos_col, pos_row, *consts)


def _attn_body(qt_ref, k_ref, vt_ref, o_ref, m_s, acc_s):
    tq = qt_ref.shape[1]
    tk = vt_ref.shape[-1]
    dv = o_ref.shape[-1]
    nchain = tq // Q_CHAIN
    per_tile = tq // tk
    nfull = pl.program_id(2) * per_tile
    m_s[...] = jnp.full(m_s.shape, -jnp.inf, F32)
    acc_s[...] = jnp.zeros(acc_s.shape, F32)

    def scores(j, c, key_minus_query):
        kb = k_ref[pl.ds(pl.multiple_of(j * tk, tk), tk), :]
        s = jnp.dot(kb, qt_ref[:, c * Q_CHAIN:(c + 1) * Q_CHAIN], preferred_element_type=F32)
        if key_minus_query is not None:
            krow = lax.broadcasted_iota(jnp.int32, s.shape, 0) + key_minus_query
            qcol = lax.broadcasted_iota(jnp.int32, s.shape, 1)
            s = jnp.where(krow <= qcol, s, -jnp.inf)
        return s

    def softmax_pv(j, c, s):
        cs = slice(c * Q_CHAIN, (c + 1) * Q_CHAIN)
        r = s
        while r.shape[0] > SUBLANES:
            half = r.shape[0] // 2
            r = jnp.maximum(r[:half], r[half:])
        m_old = m_s[:, cs]
        m_new = jnp.maximum(m_old, jnp.max(r, axis=0, keepdims=True))
        alpha = jnp.exp2(m_old - m_new)
        p = jnp.exp2(s - m_new).astype(BF16)
        m_s[:, cs] = m_new
        acc_s[:, cs] = alpha * acc_s[:, cs] + jnp.dot(vt_ref[j], p, preferred_element_type=F32)

    def run(todo):
        pending = {i: scores(*todo[i]) for i in range(min(QK_AHEAD, len(todo)))}
        for i, (j, c, _) in enumerate(todo):
            if i + QK_AHEAD < len(todo):
                pending[i + QK_AHEAD] = scores(*todo[i + QK_AHEAD])
            softmax_pv(j, c, pending.pop(i))

    def full_blocks(jj, carry):
        run([(jj * per_tile + u, c, None) for u in range(per_tile) for c in range(nchain)])
        return carry

    lax.fori_loop(0, nfull // per_tile, full_blocks, 0)

    diagonal = []
    for u in range(per_tile):
        for c in range(nchain):
            k0, q0 = u * tk, c * Q_CHAIN
            if k0 > q0 + Q_CHAIN - 1:
                continue
            diagonal.append((nfull + u, c, None if k0 + tk - 1 <= q0 else k0 - q0))
    run(diagonal)

    out_t = acc_s[:dv, :] / acc_s[dv:dv + 1, :]
    o_ref[...] = out_t.T.astype(o_ref.dtype)


def _attention(qt, k, vt):
    b, nh, dqk, s = qt.shape
    nkb, vrows, tk = vt.shape[2:]
    tq = T_Q
    return pl.pallas_call(
        _attn_body,
        grid=(b, nh, s // tq),
        in_specs=[pl.BlockSpec((None, None, dqk, tq), lambda bi, h, i: (bi, h, 0, i)),
                  pl.BlockSpec((None, None, s, dqk), lambda bi, h, i: (bi, h, 0, 0)),
                  pl.BlockSpec((None, None, nkb, vrows, tk), lambda bi, h, i: (bi, h, 0, 0, 0))],
        out_specs=pl.BlockSpec((None, tq, V_DIM), lambda bi, h, i: (bi, i, h)),
        out_shape=jax.ShapeDtypeStruct((b, s, nh * V_DIM), BF16),
        scratch_shapes=[pltpu.VMEM((1, tq), F32), pltpu.VMEM((vrows, tq), F32)],
        compiler_params=_params("parallel", "parallel", "arbitrary"),
        name="attention",
    )(qt, k, vt)


def _merge_body(o_ref, ga_ref, gb_ref, x_ref, wmo_ref, wout_ref, g2_ref, wrhl_ref, rb_ref,
                hrow_ref, rec_ref, cnt_ref, cnt_s, *, n_groups, per_group):
    @pl.when(pl.program_id(0) == 0)
    def _():
        cnt_s[...] = jnp.zeros_like(cnt_s)

    br_b = jnp.dot(o_ref[...], wmo_ref[...], preferred_element_type=F32)
    merged = ga_ref[...].astype(F32) + _sigmoid(gb_ref[...].astype(F32)) * br_b
    h = x_ref[...] + jnp.dot(merged.astype(BF16), wout_ref[...], preferred_element_type=F32)

    u = h * lax.rsqrt(jnp.mean(h * h, axis=-1, keepdims=True) + EPS) * g2_ref[...]
    u_hi = u.astype(BF16)
    u_lo = (u - u_hi.astype(F32)).astype(BF16)
    hi_terms = jnp.dot(u_hi, wrhl_ref[...], preferred_element_type=F32)
    nl = rb_ref.shape[-1]
    logits = (hi_terms[:, :nl] + hi_terms[:, nl:]
              + jnp.dot(u_lo, wrhl_ref[:, :nl], preferred_element_type=F32)) + rb_ref[...]

    t = logits.shape[0]
    lane = lax.broadcasted_iota(jnp.int32, (t, nl), 1)
    neg = -jnp.inf

    def first_argmax(z, zmax):
        return jnp.min(jnp.where(z == zmax, lane, nl), axis=-1, keepdims=True)

    zg = jnp.where(lane < n_groups, logits, neg)
    mg = jnp.max(zg, axis=-1, keepdims=True)
    g_sel = first_argmax(zg, mg)
    g_prob = 1.0 / jnp.sum(jnp.exp(zg - mg), axis=-1, keepdims=True)
    lo = n_groups + g_sel * per_group
    ze = jnp.where((lane >= lo) & (lane < lo + per_group), logits, neg)
    m1 = jnp.max(ze, axis=-1, keepdims=True)
    i1 = first_argmax(ze, m1)
    ze2 = jnp.where(lane == i1, neg, ze)
    m2 = jnp.max(ze2, axis=-1, keepdims=True)
    i2 = first_argmax(ze2, m2)
    e21 = jnp.exp(m2 - m1)
    w1 = g_prob / (1.0 + e21)
    w2 = w1 * e21

    cnt_s[...] = cnt_s[...] + jnp.sum(jnp.where(lane == g_sel, 1.0, 0.0), axis=0, keepdims=True)
    cnt_ref[...] = cnt_s[...]

    rec = jnp.where(lane == 0, (i1 - lo).astype(F32), 0.0)
    for col, val in ((1, (i2 - lo).astype(F32)), (2, w1), (3, w2), (4, g_sel.astype(F32))):
        rec = jnp.where(lane == col, val, rec)
    rec_ref[...] = rec

    _natural_to_rows(hrow_ref, h, TOKEN_SLAB)
    hrow_ref[pl.ds(ROW_CHUNKS, t, stride=TOKEN_SLAB), :] = rec
    for c in range(ROW_CHUNKS + 1, TOKEN_SLAB):
        hrow_ref[pl.ds(c, t, stride=TOKEN_SLAB), :] = jnp.zeros((t, nl), F32)


def _merge(o2, ga2, gb2, x2, wmo, wout, g2, wrhl, rb, n_groups, per_group):
    n, d = x2.shape
    t = T_MERGE
    consts = (wmo, wout, g2, wrhl, rb)
    tile = pl.BlockSpec((t, d), lambda i: (i, 0))
    return pl.pallas_call(
        functools.partial(_merge_body, n_groups=n_groups, per_group=per_group),
        grid=(n // t,),
        in_specs=[tile, tile, tile, tile] + [_const_spec(a.shape) for a in consts],
        out_specs=[pl.BlockSpec((t * TOKEN_SLAB, LANES), lambda i: (i, 0)),
                   pl.BlockSpec((t, ROUTER_LANES), lambda i: (i, 0)),
                   pl.BlockSpec((SUBLANES, ROUTER_LANES), lambda i: (0, 0))],
        out_shape=[jax.ShapeDtypeStruct((n * TOKEN_SLAB, LANES), F32),
                   jax.ShapeDtypeStruct((n, ROUTER_LANES), F32),
                   jax.ShapeDtypeStruct((SUBLANES, ROUTER_LANES), F32)],
        scratch_shapes=[pltpu.VMEM((SUBLANES, ROUTER_LANES), F32)],
        compiler_params=_params("arbitrary"),
        name="merge_router",
    )(o2, ga2, gb2, x2, *consts)


def _slab(ref, row, slab=ROW_CHUNKS):
    return ref.at[pl.ds(pl.multiple_of(row * slab, slab), slab), :]


def _experts_body(order_ref, first_ref, nvalid_ref, bg_ref, nb_ref, h_hbm, g2_ref, w1_ref, w3_ref, w2_ref, y_hbm,
                  xbuf0, xbuf1, xbuf2, ybuf0, ybuf1, ybuf2, gsem, ssem):
    blk = pl.program_id(0)
    rows = MOE_BLOCK
    n_used = nb_ref[0]
    n_tok = order_ref.shape[0]
    per_group = w1_ref.shape[0]
    xbufs = (xbuf0, xbuf1, xbuf2)
    ybufs = (ybuf0, ybuf1, ybuf2)
    nbuf = len(xbufs)

    def token(b, r):
        return order_ref[jnp.minimum(first_ref[b] + r, n_tok - 1)], r < nvalid_ref[b]

    def start_gather(b, s):
        for r in range(rows):
            tok, valid = token(b, r)
            tok = jnp.where(valid, tok, 0)
            pltpu.make_async_copy(_slab(h_hbm, tok, TOKEN_SLAB), _slab(xbufs[s], r, TOKEN_SLAB),
                                  gsem.at[s]).start(priority=r % 2)

    def wait_gather(s):
        pltpu.make_async_copy(h_hbm.at[pl.ds(0, rows * TOKEN_SLAB), :], xbufs[s], gsem.at[s]).wait()

    def start_scatter(b, s):
        for r in range(rows):
            tok, valid = token(b, r)
            dst = jnp.where(valid, tok, n_tok + r)
            pltpu.make_async_copy(_slab(ybufs[s], r), _slab(y_hbm, dst), ssem.at[s]).start(priority=r % 2)

    def wait_scatter(s):
        pltpu.make_async_copy(ybufs[s], y_hbm.at[pl.ds(0, rows * ROW_CHUNKS), :], ssem.at[s]).wait()

    @pl.when(blk == 0)
    def _():
        ybuf1[...] = jnp.zeros_like(ybuf1)
        spare = y_hbm.at[pl.ds(y_hbm.shape[0] - rows * ROW_CHUNKS, rows * ROW_CHUNKS), :]
        fill = pltpu.make_async_copy(ybuf1, spare, ssem.at[1])
        fill.start()
        fill.wait()

    for first in range(nbuf - 1):
        @pl.when((blk == 0) & (first < n_used))
        def _():
            start_gather(first, first)

    def step(s, has_ahead):
        wait_gather(s)
        if has_ahead:
            start_gather(blk + nbuf - 1, (s + nbuf - 1) % nbuf)
        x = _rows_to_natural(xbufs[s], rows, TOKEN_SLAB)
        rec = xbufs[s][pl.ds(ROW_CHUNKS, rows, stride=TOKEN_SLAB), :]
        pick = [rec[:, kk:kk + 1] for kk in range(TOP_K)]
        gate = [rec[:, TOP_K + kk:TOP_K + kk + 1] for kk in range(TOP_K)]
        u = (x * lax.rsqrt(jnp.mean(x * x, axis=-1, keepdims=True) + EPS) * g2_ref[...]).astype(BF16)
        hidden = []
        for j in range(per_group):
            a = jnp.dot(u, w1_ref[j], preferred_element_type=F32)
            c = jnp.dot(u, w3_ref[j], preferred_element_type=F32)
            g = jnp.zeros_like(gate[0])
            for kk in range(TOP_K):
                g = jnp.where(pick[kk] == float(j), gate[kk], g)
            routed = functools.reduce(jnp.logical_or, [pick[kk] == float(j) for kk in range(TOP_K)])
            hidden.append(jnp.where(routed, a * _sigmoid(a) * c * g, 0.0).astype(BF16))
        w2_all = w2_ref[...].reshape(per_group * w2_ref.shape[1], w2_ref.shape[2])
        y = x + jnp.dot(jnp.concatenate(hidden, axis=1), w2_all, preferred_element_type=F32)
        _natural_to_rows(ybufs[s], y)

        @pl.when(blk > 0)
        def _():
            wait_scatter((s + nbuf - 1) % nbuf)
        start_scatter(blk, s)

    slot = lax.rem(blk, nbuf)
    for s in range(nbuf):
        @pl.when((slot == s) & (blk + nbuf - 1 < n_used))
        def _():
            step(s, True)

        @pl.when((slot == s) & (blk + nbuf - 1 >= n_used) & (blk < n_used))
        def _():
            step(s, False)

            @pl.when(blk + 1 == n_used)
            def _():
                wait_scatter(s)


def _experts(order, block_first, block_nvalid, block_g, n_used, h_slabs, g2, w1, w3, w2):
    d = g2.shape[-1]
    rows = MOE_BLOCK
    nblk = block_g.shape[0]
    per_group, de = w1.shape[1], w1.shape[-1]
    n_out_rows = order.shape[0] + rows
    grid_spec = pltpu.PrefetchScalarGridSpec(
        num_scalar_prefetch=5,
        grid=(nblk,),
        in_specs=[
            pl.BlockSpec(memory_space=pl.ANY),
            pl.BlockSpec(g2.shape, lambda i, *_: (0, 0)),
            pl.BlockSpec((None, per_group, d, de), lambda i, order, first, nvalid, bg, nb: (bg[i], 0, 0, 0)),
            pl.BlockSpec((None, per_group, d, de), lambda i, order, first, nvalid, bg, nb: (bg[i], 0, 0, 0)),
            pl.BlockSpec((None, per_group, de, d), lambda i, order, first, nvalid, bg, nb: (bg[i], 0, 0, 0)),
        ],
        out_specs=pl.BlockSpec(memory_space=pl.ANY),
        scratch_shapes=[pltpu.VMEM((rows * TOKEN_SLAB, LANES), F32)] * 3
        + [pltpu.VMEM((rows * ROW_CHUNKS, LANES), F32)] * 3
        + [pltpu.SemaphoreType.DMA((3,)), pltpu.SemaphoreType.DMA((3,))],
    )
    return pl.pallas_call(
        _experts_body,
        grid_spec=grid_spec,
        out_shape=jax.ShapeDtypeStruct((n_out_rows * ROW_CHUNKS, LANES), F32),
        compiler_params=_params("arbitrary"),
        name="experts",
    )(order, block_first, block_nvalid, block_g, n_used, h_slabs, g2, w1, w3, w2)


def _unrow_body(rows_ref, out_ref):
    out_ref[...] = _rows_to_natural(rows_ref, out_ref.shape[0])


def _unrow(out_rows, n):
    d = ROW_CHUNKS * LANES
    t = T_COMBINE
    return pl.pallas_call(
        _unrow_body,
        grid=(n // t,),
        in_specs=[pl.BlockSpec((t * ROW_CHUNKS, LANES), lambda i: (i, 0))],
        out_specs=pl.BlockSpec((t, d), lambda i: (i, 0)),
        out_shape=jax.ShapeDtypeStruct((n, d), F32),
        compiler_params=_params("parallel"),
        name="unrow",
    )(out_rows)


def _dispatch_tables(flat_e, counts):
    n_experts = counts.shape[0]
    m = flat_e.shape[0]
    m_pad = m + n_experts * MOE_BLOCK
    n_blocks = m_pad // MOE_BLOCK
    padded = ((counts + MOE_BLOCK - 1) // MOE_BLOCK) * MOE_BLOCK
    expert = jnp.arange(n_experts, dtype=jnp.int32)
    upto = expert[None, :] <= expert[:, None]
    start = jnp.sum(jnp.where(upto, counts[None, :], 0), axis=1) - counts
    pad_end = jnp.sum(jnp.where(upto, padded[None, :], 0), axis=1)
    pad_start = pad_end - padded
    block_row = jnp.arange(n_blocks, dtype=jnp.int32) * MOE_BLOCK
    block_e = jnp.minimum(jnp.sum(pad_end[None, :] <= block_row[:, None], axis=1), n_experts - 1).astype(jnp.int32)
    n_used = (jnp.sum(padded) // MOE_BLOCK).astype(jnp.int32).reshape(1)
    mine = block_e[:, None] == expert[None, :]

    def of_block(table):
        return jnp.sum(jnp.where(mine, table[None, :], 0), axis=1)

    order = jnp.argsort(flat_e, stable=True).astype(jnp.int32)
    off = block_row - of_block(pad_start)
    block_first = (of_block(start) + off).astype(jnp.int32)
    block_nvalid = jnp.clip(of_block(counts) - off, 0, MOE_BLOCK).astype(jnp.int32)
    return order, block_first, block_nvalid, block_e, n_used


def _layer(x, positions, norm1_g, w_in, conv_w, conv_b, lru_wa, lru_ba, lru_wx, lru_bx, lru_lambda, w_rnn_o,
           q_norm_g, w_uq, kv_norm_g, w_ukv, qk_norm_q_g, qk_norm_k_g, w_mla_o, w_out, norm2_g,
           router_wg, router_bg, router_we, router_be, exp_w1, exp_w3, exp_w2):
    b, s, d = x.shape
    n = b * s
    d_rnn = conv_w.shape[-1]
    q_lora = q_norm_g.shape[-1]
    kv_lora = kv_norm_g.shape[-1]
    nh, half = N_HEADS, QK_ROPE // 2
    row = lambda a: a.reshape(1, -1).astype(F32)
    col = lambda a: a.reshape(-1, 1).astype(F32)

    o_xr, o_yr, o_cq, o_ckv, o_kpe, o_ga, o_gb = np.cumsum([0, d_rnn, d_rnn, q_lora, kv_lora, QK_ROPE, d])
    w_kpe = w_in[:, o_kpe:o_kpe + QK_ROPE]
    w_kpe_rot = jnp.concatenate([-w_kpe[:, half:], w_kpe[:, :half]], axis=1)
    w_in_k = jnp.concatenate([w_in[:, :o_kpe], w_in[:, o_ga:], w_kpe, w_kpe_rot], axis=1).astype(BF16)
    widths = (d_rnn, d_rnn, q_lora, kv_lora, d, d, 2 * QK_ROPE)
    dtypes = (BF16, BF16, F32, F32, BF16, BF16, F32)

    x2 = x.reshape(n, d)
    xr, yr, cq, ckv, ga, gb, kpe2 = _in_proj(x2, row(norm1_g), w_in_k, widths, dtypes)

    w_gate = jnp.concatenate([lru_wa, lru_wx], axis=-1).astype(BF16)
    gated_a = _rglru(xr.reshape(b, s, d_rnn), yr.reshape(b, s, d_rnn), ga.reshape(b, s, d),
                     conv_w.astype(F32), row(conv_b), w_gate, row(lru_ba), row(lru_bx), row(lru_lambda),
                     w_rnn_o.astype(BF16))

    def rot(v):
        return jnp.concatenate([-v[..., half:], v[..., :half]], axis=-1)

    def swap(v):
        return jnp.concatenate([v[..., half:], v[..., :half]], axis=-1)

    wq = w_uq.reshape(q_lora, nh, QK_DIM).transpose(1, 0, 2)
    wq_t = wq.transpose(0, 2, 1).astype(BF16)
    wqr_t = rot(wq[..., QK_NOPE:]).transpose(0, 2, 1).astype(BF16)
    wkv = w_ukv.reshape(kv_lora, nh, QK_NOPE + V_DIM).transpose(1, 0, 2)
    wk = wkv[..., :QK_NOPE].astype(BF16)
    wv_t = wkv[..., QK_NOPE:].transpose(0, 2, 1).astype(BF16)
    gq = qk_norm_q_g.astype(F32)
    gk = qk_norm_k_g.astype(F32)
    inv_freq = ROPE_THETA ** (-jnp.arange(0, QK_ROPE, 2, dtype=F32) / QK_ROPE)
    freq = jnp.concatenate([inv_freq, inv_freq])
    consts = (row(q_norm_g), row(kv_norm_g), wq_t, wqr_t, wk, wv_t,
              col(gq[:QK_NOPE]), col(gq[QK_NOPE:]), col(swap(gq[QK_NOPE:])),
              row(gk[:QK_NOPE]), row(gk[QK_NOPE:]), row(swap(gk[QK_NOPE:])), row(freq), col(freq))
    pos = positions.astype(F32)
    qt, k, vt = _mla_proj(cq.reshape(b, s, q_lora), ckv.reshape(b, s, kv_lora), kpe2.reshape(b, s, 2 * QK_ROPE),
                          pos.reshape(b, s, 1), pos.reshape(b, 1, s), consts)
    o = _attention(qt, k, vt)

    n_groups = router_wg.shape[-1]
    n_experts = router_we.shape[-1]
    per_group = n_experts // n_groups
    wr = jnp.concatenate([router_wg, router_we,
                          jnp.zeros((d, ROUTER_LANES - n_groups - n_experts), F32)], axis=1).astype(F32)
    wr_hi = wr.astype(BF16)
    wr_lo = (wr - wr_hi.astype(F32)).astype(BF16)
    rb = jnp.concatenate([router_bg, router_be,
                          jnp.zeros((ROUTER_LANES - n_groups - n_experts,), F32)]).reshape(1, ROUTER_LANES).astype(F32)
    h_slabs, rec, cnt = _merge(o.reshape(n, d), gated_a.reshape(n, d), gb, x2, w_mla_o.astype(BF16),
                               w_out.astype(BF16), row(norm2_g), jnp.concatenate([wr_hi, wr_lo], axis=1), rb,
                               n_groups, per_group)

    group_id = rec[:, 2 * TOP_K].astype(jnp.int32)
    counts = cnt[0, :n_groups].astype(jnp.int32)
    order, block_first, block_nvalid, block_g, n_used = _dispatch_tables(group_id, counts)
    de = exp_w1.shape[-1]
    out_rows = _experts(order, block_first, block_nvalid, block_g, n_used, h_slabs, row(norm2_g),
                        exp_w1.astype(BF16).reshape(n_groups, per_group, d, de),
                        exp_w3.astype(BF16).reshape(n_groups, per_group, d, de),
                        exp_w2.astype(BF16).reshape(n_groups, per_group, de, d))
    return _unrow(out_rows, n).reshape(b, s, d)


def kernel(x, positions, norm1_g, w_in, conv_w, conv_b, lru_wa, lru_ba, lru_wx, lru_bx, lru_lambda, w_rnn_o, q_norm_g, w_uq, kv_norm_g, w_ukv, qk_norm_q_g, qk_norm_k_g, w_mla_o, w_out, norm2_g, router_wg, router_bg, router_we, router_be, exp_w1, exp_w3, exp_w2):
    args = (norm1_g, w_in, conv_w, conv_b, lru_wa, lru_ba, lru_wx, lru_bx, lru_lambda, w_rnn_o, q_norm_g, w_uq,
            kv_norm_g, w_ukv, qk_norm_q_g, qk_norm_k_g, w_mla_o, w_out, norm2_g, router_wg, router_bg, router_we,
            router_be, exp_w1, exp_w3, exp_w2)
    depth = norm1_g.shape[0]
    h = x
    for l in range(depth):
        h = _layer(h, positions, *[a[l] for a in args])
    return h
```
